```python
import jax, jax.numpy as jnp
from jax import lax
import numpy as np

D_MODEL = 2048
BATCH = 2
SEQ = 8192
DEPTH = 2

MIX_WIDTH = 2 * D_MODEL
A_WIDTH = D_MODEL // 1 if False else MIX_WIDTH // 2
A_GROUPS = 8
A_CHUNK = 128
B_WIDTH = MIX_WIDTH // 2
B_HEAD_DIM = 64
B_HEADS = B_WIDTH // B_HEAD_DIM
B_GROUPS = 8
B_STATE = 128
B_CONV = 4
B_CHUNK = 128
B_XBC = B_WIDTH + 2 * B_GROUPS * B_STATE
C_WIDTH = MIX_WIDTH // 2
C_CONV = 3
D_WIDTH = MIX_WIDTH // 2
D_HEAD_DIM = 128
D_HEADS = D_WIDTH // D_HEAD_DIM
D_PATTERNS = ((128, 1), (512, 4), (2048, 16))

EPS = 1e-5
N_EVEN = (DEPTH + 1) // 2
N_ODD = DEPTH // 2
IN_EVEN = 3 * A_WIDTH + B_WIDTH + B_XBC + B_HEADS
IN_ODD = 4 * C_WIDTH + 4 * D_WIDTH

kernel_name = "hybrid_gmlp_ssd_shortconv_dilated_attn"


def rms_norm(x, g):
    xf = x.astype(jnp.float32)
    y = xf * lax.rsqrt(jnp.mean(xf * xf, axis=-1, keepdims=True) + EPS)
    return (y * g.astype(jnp.float32)).astype(x.dtype)


def layer_norm(x, g, b):
    xf = x.astype(jnp.float32)
    mu = jnp.mean(xf, axis=-1, keepdims=True)
    xc = xf - mu
    y = xc * lax.rsqrt(jnp.mean(xc * xc, axis=-1, keepdims=True) + EPS)
    return (y * g.astype(jnp.float32) + b.astype(jnp.float32)).astype(x.dtype)


def causal_dwconv(x, w):
    K, C = w.shape
    return lax.conv_general_dilated(
        x, w[:, None, :].astype(x.dtype), window_strides=(1,),
        padding=[(K - 1, 0)], dimension_numbers=("NWC", "WIO", "NWC"),
        feature_group_count=C)


def gmlp_branch(h, ln_g, ln_b, ws, bs):
    Bb, S, _ = h.shape
    u, v, z = jnp.split(h, 3, axis=-1)
    v = layer_norm(v, ln_g, ln_b)
    G, Q, _ = ws.shape
    causal = jnp.tril(jnp.ones((Q, Q), dtype=bool))
    ws_c = jnp.where(causal, ws, jnp.zeros_like(ws))
    vc = v.reshape(Bb, S // Q, Q, G, A_WIDTH // G)
    mixed = jnp.einsum("gts,bcsgd->bctgd", ws_c, vc) + bs.T[None, None, :, :, None]
    return jax.nn.silu(z) * (u * mixed.reshape(Bb, S, A_WIDTH))


def segsum(x):
    T = x.shape[-1]
    cs = jnp.cumsum(x, axis=-1)
    seg = cs[..., :, None] - cs[..., None, :]
    return jnp.where(jnp.tril(jnp.ones((T, T), dtype=bool)), seg, -jnp.inf)


def ssd_scan(x, dt, a, bm, cm):
    Bb, S, H, P = x.shape
    G, N = bm.shape[2], bm.shape[3]
    R = H // G
    Q = B_CHUNK
    nc = S // Q
    xdt = (x * dt[..., None]).reshape(Bb, nc, Q, G, R, P)
    adt = (dt * a).reshape(Bb, nc, Q, G, R).transpose(0, 3, 4, 1, 2)
    bc = bm.reshape(Bb, nc, Q, G, N)
    cc = cm.reshape(Bb, nc, Q, G, N)
    a_cs = jnp.cumsum(adt, axis=-1)
    L = jnp.exp(segsum(adt))
    cb = jnp.einsum("bclgn,bcsgn->bcgls", cc, bc)
    y_diag = jnp.einsum("bcgls,bgrcls,bcsgrp->bclgrp", cb, L, xdt)
    decay_states = jnp.exp(a_cs[..., -1:] - a_cs)
    states = jnp.einsum("bclgn,bgrcl,bclgrp->bcgrpn", bc, decay_states, xdt)
    chunk_decay = jnp.exp(a_cs[..., -1])

    def step(hstate, inp):
        s_c, d_c = inp
        return hstate * d_c[..., None, None] + s_c, hstate

    _, prev = lax.scan(step, jnp.zeros_like(states[:, 0]),
                       (jnp.moveaxis(states, 1, 0), jnp.moveaxis(chunk_decay, -1, 0)))
    prev = jnp.moveaxis(prev, 0, 1)
    y_off = jnp.einsum("bclgn,bcgrpn,bgrcl->bclgrp", cc, prev, jnp.exp(a_cs))
    return (y_diag + y_off).reshape(Bb, S, H, P)


def ssd_branch(h, conv_w, conv_b, dt_bias, a_log, d_skip, norm_g):
    Bb, S, _ = h.shape
    z = h[..., :B_WIDTH]
    xbc = h[..., B_WIDTH:B_WIDTH + B_XBC]
    dt_raw = h[..., B_WIDTH + B_XBC:]
    xbc = jax.nn.silu(causal_dwconv(xbc, conv_w) + conv_b.astype(xbc.dtype))
    gn = B_GROUPS * B_STATE
    xs = xbc[..., :B_WIDTH].astype(jnp.float32).reshape(Bb, S, B_HEADS, B_HEAD_DIM)
    bm = xbc[..., B_WIDTH:B_WIDTH + gn].astype(jnp.float32).reshape(Bb, S, B_GROUPS, B_STATE)
    cm = xbc[..., B_WIDTH + gn:].astype(jnp.float32).reshape(Bb, S, B_GROUPS, B_STATE)
    dt = jax.nn.softplus(dt_raw.astype(jnp.float32) + dt_bias.astype(jnp.float32))
    a = -jnp.exp(a_log.astype(jnp.float32))
    y = ssd_scan(xs, dt, a, bm, cm) + d_skip.astype(jnp.float32)[:, None] * xs
    y = y.reshape(Bb, S, B_WIDTH) * jax.nn.silu(z.astype(jnp.float32))
    yg = y.reshape(Bb, S, B_GROUPS, B_WIDTH // B_GROUPS)
    yg = yg * lax.rsqrt(jnp.mean(yg * yg, axis=-1, keepdims=True) + EPS)
    return (yg.reshape(Bb, S, B_WIDTH) * norm_g.astype(jnp.float32)).astype(h.dtype)


def shortconv_branch(h, conv_w):
    bg, cg, hx, z = jnp.split(h, 4, axis=-1)
    return jax.nn.silu(z) * (bg * causal_dwconv(cg * hx, conv_w))


def dilated_window_attention(q, k, v, dil, n_back):
    Bb, S, H, E = q.shape
    M = S // dil
    nb = -(-M // n_back)
    Mp = nb * n_back

    def to_blocks(t):
        t = t.reshape(Bb, M, dil, H, E)
        t = jnp.pad(t, ((0, 0), (0, Mp - M), (0, 0), (0, 0), (0, 0)))
        return t.reshape(Bb, nb, n_back, dil, H, E)

    def with_prev(t):
        prev = jnp.pad(t, ((0, 0), (1, 0), (0, 0), (0, 0), (0, 0), (0, 0)))[:, :-1]
        return jnp.concatenate([prev, t], axis=2)

    qb = to_blocks(q)
    kw = with_prev(to_blocks(k))
    vw = with_prev(to_blocks(v))
    s = jnp.einsum("bnarhe,bnjrhe->bnrhaj", qb, kw,
                   preferred_element_type=jnp.float32) * (E ** -0.5)
    a_idx = jnp.arange(n_back)[:, None]
    j_idx = jnp.arange(2 * n_back)[None, :]
    band = (j_idx >= a_idx) & (j_idx <= a_idx + n_back)
    key_ok = (jnp.arange(nb)[:, None] > 0) | (j_idx >= n_back)
    mask = (band[None] & key_ok[:, None, :])[None, :, None, None]
    s = jnp.where(mask, s, -jnp.inf)
    mx = jnp.max(s, axis=-1, keepdims=True)
    p = jnp.exp(s - mx)
    l = jnp.sum(p, axis=-1, keepdims=True)
    o = jnp.einsum("bnrhaj,bnjrhe->bnarhe", p / l, vw.astype(jnp.float32))
    lse = (mx + jnp.log(l))[..., 0].transpose(0, 1, 4, 2, 3)
    o = o.reshape(Bb, Mp, dil, H, E)[:, :M].reshape(Bb, S, H, E)
    lse = lse.reshape(Bb, Mp, dil, H)[:, :M].reshape(Bb, S, H)
    return o, lse


def dilated_attention_branch(h):
    Bb, S, _ = h.shape
    q, k, v, z = jnp.split(h, 4, axis=-1)
    q = q.reshape(Bb, S, D_HEADS, D_HEAD_DIM)
    k = k.reshape(Bb, S, D_HEADS, D_HEAD_DIM)
    v = v.reshape(Bb, S, D_HEADS, D_HEAD_DIM)
    outs, lses = [], []
    for window, dil in D_PATTERNS:
        o, lse = dilated_window_attention(q, k, v, dil, window // dil)
        outs.append(o)
        lses.append(lse)
    wts = jax.nn.softmax(jnp.stack(lses, axis=0), axis=0)
    o = jnp.sum(wts[..., None] * jnp.stack(outs, axis=0), axis=0)
    return jax.nn.silu(z) * o.reshape(Bb, S, D_WIDTH).astype(h.dtype)


def even_layer(x, norm_g, w_in, ln_g, ln_b, ws, bs, conv_w, conv_b, dt_bias,
               a_log, d_skip, ssd_norm_g, w_out):
    h = jnp.einsum("bsd,df->bsf", rms_norm(x, norm_g), w_in)
    ya = gmlp_branch(h[..., :3 * A_WIDTH], ln_g, ln_b, ws, bs)
    yb = ssd_branch(h[..., 3 * A_WIDTH:], conv_w, conv_b, dt_bias, a_log, d_skip, ssd_norm_g)
    y = jnp.concatenate([ya, yb.astype(ya.dtype)], axis=-1)
    return x + jnp.einsum("bsf,fd->bsd", y, w_out).astype(x.dtype)


def odd_layer(x, norm_g, w_in, sconv_w, w_out):
    h = jnp.einsum("bsd,df->bsf", rms_norm(x, norm_g), w_in)
    yc = shortconv_branch(h[..., :4 * C_WIDTH], sconv_w)
    yd = dilated_attention_branch(h[..., 4 * C_WIDTH:])
    y = jnp.concatenate([yc, yd.astype(yc.dtype)], axis=-1)
    return x + jnp.einsum("bsf,fd->bsd", y, w_out).astype(x.dtype)


def setup_inputs(seed: int = 0) -> dict:
    key = jax.random.key(seed)
    ks = jax.random.split(key, 20)
    f32 = jnp.float32
    nrm = lambda k, shape, scale: jax.random.normal(k, shape, f32) * scale
    x = jax.random.normal(ks[0], (BATCH, SEQ, D_MODEL), f32)
    even_norm_g = 1.0 + nrm(ks[1], (N_EVEN, D_MODEL), 0.02)
    even_w_in = nrm(ks[2], (N_EVEN, D_MODEL, IN_EVEN), D_MODEL ** -0.5)
    gmlp_ln_g = 1.0 + nrm(ks[3], (N_EVEN, A_WIDTH), 0.02)
    gmlp_ln_b = nrm(ks[4], (N_EVEN, A_WIDTH), 0.02)
    gmlp_ws = nrm(ks[5], (N_EVEN, A_GROUPS, A_CHUNK, A_CHUNK), A_CHUNK ** -0.5)
    gmlp_bs = 1.0 + nrm(ks[6], (N_EVEN, A_GROUPS, A_CHUNK), 0.1)
    ssd_conv_w = nrm(ks[7], (N_EVEN, B_CONV, B_XBC), B_CONV ** -0.5)
    ssd_conv_b = nrm(ks[8], (N_EVEN, B_XBC), 0.02)
    dt0 = jnp.exp(jax.random.uniform(ks[9], (N_EVEN, B_HEADS), f32,
                                     np.log(1e-3).astype(np.float32), np.log(1e-1).astype(np.float32)))
    ssd_dt_bias = dt0 + jnp.log(-jnp.expm1(-dt0))
    ssd_a_log = jnp.log(jax.random.uniform(ks[10], (N_EVEN, B_HEADS), f32, 1.0, 16.0))
    ssd_d = 1.0 + nrm(ks[11], (N_EVEN, B_HEADS), 0.1)
    ssd_norm_g = 1.0 + nrm(ks[12], (N_EVEN, B_WIDTH), 0.02)
    even_w_out = nrm(ks[13], (N_EVEN, MIX_WIDTH, D_MODEL), MIX_WIDTH ** -0.5)
    odd_norm_g = 1.0 + nrm(ks[14], (N_ODD, D_MODEL), 0.02)
    odd_w_in = nrm(ks[15], (N_ODD, D_MODEL, IN_ODD), D_MODEL ** -0.5)
    sconv_w = nrm(ks[16], (N_ODD, C_CONV, C_WIDTH), C_CONV ** -0.5)
    odd_w_out = nrm(ks[17], (N_ODD, MIX_WIDTH, D_MODEL), MIX_WIDTH ** -0.5)
    final_norm_g = 1.0 + nrm(ks[18], (D_MODEL,), 0.02)
    return {"x": x, "even_norm_g": even_norm_g, "even_w_in": even_w_in,
            "gmlp_ln_g": gmlp_ln_g, "gmlp_ln_b": gmlp_ln_b, "gmlp_ws": gmlp_ws,
            "gmlp_bs": gmlp_bs, "ssd_conv_w": ssd_conv_w, "ssd_conv_b": ssd_conv_b,
            "ssd_dt_bias": ssd_dt_bias, "ssd_a_log": ssd_a_log, "ssd_d": ssd_d,
            "ssd_norm_g": ssd_norm_g, "even_w_out": even_w_out,
            "odd_norm_g": odd_norm_g, "odd_w_in": odd_w_in, "sconv_w": sconv_w,
            "odd_w_out": odd_w_out, "final_norm_g": final_norm_g}


def reference(x, even_norm_g, even_w_in, gmlp_ln_g, gmlp_ln_b, gmlp_ws, gmlp_bs,
              ssd_conv_w, ssd_conv_b, ssd_dt_bias, ssd_a_log, ssd_d, ssd_norm_g,
              even_w_out, odd_norm_g, odd_w_in, sconv_w, odd_w_out, final_norm_g):
    for layer in range(DEPTH):
        i = layer // 2
        if layer % 2 == 0:
            x = even_layer(x, even_norm_g[i], even_w_in[i], gmlp_ln_g[i], gmlp_ln_b[i],
                           gmlp_ws[i], gmlp_bs[i], ssd_conv_w[i], ssd_conv_b[i],
                           ssd_dt_bias[i], ssd_a_log[i], ssd_d[i], ssd_norm_g[i],
                           even_w_out[i])
        else:
            x = odd_layer(x, odd_norm_g[i], odd_w_in[i], sconv_w[i], odd_w_out[i])
    return rms_norm(x, final_norm_g)
```

```python
import functools

import jax
import jax.numpy as jnp
from jax import lax
from jax.experimental import pallas as pl
from jax.experimental.pallas import tpu as pltpu

F32 = jnp.float32
BF16 = jnp.bfloat16

EPS = 1e-5
D_MODEL = 2048
WIDTH = 2048
A_GROUPS = 8
CHUNK = 128
B_HEAD_DIM = 64
B_HEADS = WIDTH // B_HEAD_DIM
B_GROUPS = 8
B_STATE = 128
B_CONV = 4
B_XBC = WIDTH + 2 * B_GROUPS * B_STATE
HEADS_PER_GROUP = B_HEADS // B_GROUPS
GROUP_W = WIDTH // B_GROUPS
C_CONV = 3
D_HEAD_DIM = 128
D_HEADS = WIDTH // D_HEAD_DIM
N_BACK = 128
DILATIONS = (1, 4, 16)
SUPER = N_BACK * DILATIONS[-1]
LANES = 128
SUBLANES = 8
VMEM_LIMIT = 56 * 1024 * 1024


def _cparams(*sem):
    return pltpu.CompilerParams(dimension_semantics=sem, vmem_limit_bytes=VMEM_LIMIT)


def _silu(z):
    return z * jax.nn.sigmoid(z)


def _rmsnorm_kernel(x_ref, g_ref, o_ref):
    x = x_ref[...]
    ms = jnp.mean(x * x, axis=-1, keepdims=True)
    o_ref[...] = (x * lax.rsqrt(ms + EPS) * g_ref[...]).astype(o_ref.dtype)


def _rmsnorm(x2d, g, out_dtype, tm=512):
    t, d = x2d.shape
    return pl.pallas_call(
        _rmsnorm_kernel,
        grid=(t // tm,),
        in_specs=[pl.BlockSpec((tm, d), lambda i: (i, 0)),
                  pl.BlockSpec((1, d), lambda i: (0, 0))],
        out_specs=pl.BlockSpec((tm, d), lambda i: (i, 0)),
        out_shape=jax.ShapeDtypeStruct((t, d), out_dtype),
        compiler_params=_cparams("arbitrary"),
    )(x2d, g.reshape(1, d))


def _matmul_kernel(a_ref, w_ref, *rest, has_res, cast_rows):
    if has_res:
        r_ref, o_ref, wb_ref = rest
    else:
        o_ref, wb_ref = rest

    @pl.when(pl.program_id(1) == 0)
    def _():
        def body(k, carry):
            r = pl.multiple_of(k * cast_rows, cast_rows)
            wb_ref[pl.ds(r, cast_rows), :] = w_ref[pl.ds(r, cast_rows), :].astype(BF16)
            return carry
        lax.fori_loop(0, w_ref.shape[0] // cast_rows, body, 0)

    acc = jnp.dot(a_ref[...], wb_ref[...], preferred_element_type=F32)
    if has_res:
        acc = acc + r_ref[...]
    o_ref[...] = acc.astype(o_ref.dtype)


def _matmul(a, w, out_dtype, *, n_cols=None, res=None, tm=1024, tn=1024):
    m, k = a.shape
    n = w.shape[1] if n_cols is None else n_cols
    in_specs = [pl.BlockSpec((tm, k), lambda j, i: (i, 0)),
                pl.BlockSpec((k, tn), lambda j, i: (0, j))]
    args = [a, w]
    if res is not None:
        in_specs.append(pl.BlockSpec((tm, tn), lambda j, i: (i, j)))
        args.append(res)
    return pl.pallas_call(
        functools.partial(_matmul_kernel, has_res=res is not None, cast_rows=256),
        grid=(n // tn, m // tm),
        in_specs=in_specs,
        out_specs=pl.BlockSpec((tm, tn), lambda j, i: (i, j)),
        out_shape=jax.ShapeDtypeStruct((m, n), out_dtype),
        scratch_shapes=[pltpu.VMEM((k, tn), BF16)],
        compiler_params=_cparams("arbitrary", "arbitrary"),
    )(*args)


def _small_matmul_kernel(a_ref, w_ref, o_ref):
    o_ref[...] = jnp.dot(a_ref[...], w_ref[...].astype(BF16), preferred_element_type=F32)


def _small_matmul(a, w, tm=1024):
    m, k = a.shape
    n = w.shape[1]
    return pl.pallas_call(
        _small_matmul_kernel,
        grid=(m // tm,),
        in_specs=[pl.BlockSpec((tm, k), lambda i: (i, 0)),
                  pl.BlockSpec((k, n), lambda i: (0, 0))],
        out_specs=pl.BlockSpec((tm, n), lambda i: (i, 0)),
        out_shape=jax.ShapeDtypeStruct((m, n), F32),
        compiler_params=_cparams("arbitrary"),
    )(a, w)


def _softplus(x):
    return jnp.maximum(x, 0.0) + jnp.log1p(jnp.exp(-jnp.abs(x)))


def _split3(x):
    hi = x.astype(BF16)
    r1 = x - hi.astype(F32)
    mid = r1.astype(BF16)
    lo = (r1 - mid.astype(F32)).astype(BF16)
    return hi, mid, lo


def _mix0_kernel(u_ref, v_ref, za_ref, zb_ref, xbc_ref, dt_ref,
                 lng_ref, lnb_ref, ws_ref, bst_ref, cw_ref, cbias_ref, dtb_ref, alog_ref, dexp_ref, ng_ref,
                 o_ref,
                 wsb_ref, tail_ref, xs_ref, bm_ref, cm_ref, y_ref, st_ref):
    q = CHUNK
    row = lax.broadcasted_iota(jnp.int32, (q, q), 0)
    col = lax.broadcasted_iota(jnp.int32, (q, q), 1)
    causal = col <= row

    @pl.when(pl.program_id(1) == 0)
    def _():
        tail_ref[...] = jnp.zeros_like(tail_ref)
        st_ref[...] = jnp.zeros_like(st_ref)
        for g in range(A_GROUPS):
            wsb_ref[g] = jnp.where(causal, ws_ref[g], 0.0).astype(BF16)

    v = v_ref[...].astype(F32)
    mu = jnp.mean(v, axis=-1, keepdims=True)
    xc = v - mu
    var = jnp.mean(xc * xc, axis=-1, keepdims=True)
    rstd = lax.rsqrt(var + EPS)
    for g in range(A_GROUPS):
        sl = slice(g * GROUP_W, (g + 1) * GROUP_W)
        vg = v_ref[:, sl].astype(F32)
        vn = ((vg - mu) * rstd * lng_ref[:, sl] + lnb_ref[:, sl]).astype(BF16)
        mixed = jnp.dot(wsb_ref[g], vn, preferred_element_type=F32) + bst_ref[:, g:g + 1]
        z = za_ref[:, sl].astype(F32)
        u = u_ref[:, sl].astype(F32)
        o_ref[:, sl] = (_silu(z) * (u * mixed)).astype(o_ref.dtype)

    cwid = 512
    r8 = lax.broadcasted_iota(jnp.int32, (SUBLANES, cwid), 0)
    for j in range(B_XBC // cwid):
        sl = slice(j * cwid, (j + 1) * cwid)
        x = xbc_ref[:, sl].astype(F32)
        tl = tail_ref[:, sl]
        acc = x * cw_ref[B_CONV - 1:B_CONV, sl] + cbias_ref[:, sl]
        for k in range(1, B_CONV):
            xr = pltpu.roll(x, k, axis=0)
            tr = pltpu.roll(tl, k, axis=0)
            first = jnp.where(r8 < k, tr, xr[0:SUBLANES])
            xk = jnp.concatenate([first, xr[SUBLANES:]], axis=0)
            acc = acc + xk * cw_ref[B_CONV - 1 - k:B_CONV - k, sl]
        tail_ref[:, sl] = x[q - SUBLANES:q]
        act = _silu(acc)
        lo = j * cwid
        if lo < WIDTH:
            xs_ref[:, lo:lo + cwid] = act
        elif lo < WIDTH + B_GROUPS * B_STATE:
            bm_ref[:, lo - WIDTH:lo - WIDTH + cwid] = act
        else:
            off = lo - WIDTH - B_GROUPS * B_STATE
            cm_ref[:, off:off + cwid] = act

    dt = _softplus(dt_ref[...] + dtb_ref[...])
    adt = dt * (-jnp.exp(alog_ref[...]))
    tril = jnp.where(causal, 1.0, 0.0).astype(BF16)
    a_cs = sum(jnp.dot(tril, part, preferred_element_type=F32) for part in _split3(adt))
    a_cs_t = a_cs.T
    dt_t = dt.T

    for g in range(B_GROUPS):
        gs = slice(g * B_STATE, (g + 1) * B_STATE)
        bg = bm_ref[:, gs]
        cgb = cm_ref[:, gs].astype(BF16)
        bg_t = bg.T
        cb = lax.dot_general(cgb, bg.astype(BF16), (((1,), (1,)), ((), ())),
                             preferred_element_type=F32)
        for r in range(HEADS_PER_GROUP):
            h = g * HEADS_PER_GROUP + r
            hs = slice(h * B_HEAD_DIM, (h + 1) * B_HEAD_DIM)
            colv = a_cs[:, h:h + 1]
            rowv = a_cs_t[h:h + 1, :]
            dtr = dt_t[h:h + 1, :]
            decay = jnp.exp(jnp.where(causal, colv - rowv, -jnp.inf))
            mh = (cb * decay * dtr).astype(BF16)
            xh = xs_ref[:, hs]
            xhb = xh.astype(BF16)
            state = st_ref[h]
            y = jnp.dot(mh, xhb, preferred_element_type=F32)
            y = y + jnp.exp(colv) * jnp.dot(cgb, state.astype(BF16), preferred_element_type=F32)
            y_ref[:, hs] = y + dexp_ref[:, hs] * xh
            last = a_cs_t[h:h + 1, q - 1:q]
            wrow = jnp.exp(last - rowv) * dtr
            new = jnp.dot((bg_t * wrow).astype(BF16), xhb, preferred_element_type=F32)
            st_ref[h] = state * jnp.exp(last) + new

    for g in range(B_GROUPS):
        sl = slice(g * GROUP_W, (g + 1) * GROUP_W)
        y = y_ref[:, sl] * _silu(zb_ref[:, sl].astype(F32))
        ms = jnp.mean(y * y, axis=-1, keepdims=True)
        o_ref[:, WIDTH + g * GROUP_W:WIDTH + (g + 1) * GROUP_W] = (
            y * lax.rsqrt(ms + EPS) * ng_ref[:, sl]).astype(o_ref.dtype)


def _mix0(h0, dt_raw, batch, seq, lng, lnb, ws, bs, conv_w, conv_b, dt_bias, a_log, d_skip, norm_g):
    t = batch * seq
    nc = seq // CHUNK
    pad = LANES - B_HEADS
    row = lambda a: a.reshape(1, -1)
    params = [row(lng), row(lnb), ws, bs.T, conv_w, row(conv_b),
              row(jnp.pad(dt_bias, (0, pad))), row(jnp.pad(a_log, (0, pad))),
              row(jnp.repeat(d_skip, B_HEAD_DIM)), row(norm_g)]
    tok = lambda cb: (lambda b, c: (b * nc + c, cb))
    full = lambda a: pl.BlockSpec(a.shape, lambda b, c: (0,) * a.ndim)
    in_specs = [pl.BlockSpec((CHUNK, WIDTH), tok(0)),
                pl.BlockSpec((CHUNK, WIDTH), tok(1)),
                pl.BlockSpec((CHUNK, WIDTH), tok(2)),
                pl.BlockSpec((CHUNK, WIDTH), tok(3)),
                pl.BlockSpec((CHUNK, B_XBC), tok(2)),
                pl.BlockSpec((CHUNK, LANES), tok(0))]
    in_specs += [full(p) for p in params]
    return pl.pallas_call(
        _mix0_kernel,
        grid=(batch, nc),
        in_specs=in_specs,
        out_specs=pl.BlockSpec((CHUNK, 2 * WIDTH), tok(0)),
        out_shape=jax.ShapeDtypeStruct((t, 2 * WIDTH), BF16),
        scratch_shapes=[pltpu.VMEM((A_GROUPS, CHUNK, CHUNK), BF16),
                        pltpu.VMEM((SUBLANES, B_XBC), F32),
                        pltpu.VMEM((CHUNK, WIDTH), F32),
                        pltpu.VMEM((CHUNK, B_GROUPS * B_STATE), F32),
                        pltpu.VMEM((CHUNK, B_GROUPS * B_STATE), F32),
                        pltpu.VMEM((CHUNK, WIDTH), F32),
                        pltpu.VMEM((B_HEADS, B_STATE, B_HEAD_DIM), F32)],
        compiler_params=_cparams("arbitrary", "arbitrary"),
    )(h0, h0, h0, h0, h0, dt_raw, *params)


def _perm_kernel(x_ref, o4_ref, o16_ref, slab_ref):
    n_slabs = x_ref.shape[1] // LANES
    for j in range(n_slabs):
        cs = slice(j * LANES, (j + 1) * LANES)
        slab_ref[j] = x_ref[:, cs].astype(F32)
        for d, o_ref in ((DILATIONS[1], o4_ref), (DILATIONS[2], o16_ref)):
            rows = SUPER // d
            for r in range(d):
                o_ref[r * rows:(r + 1) * rows, cs] = slab_ref[j, pl.ds(r, rows, stride=d), :].astype(BF16)


def _permute_qkv(h1, col0, tw=512):
    t = h1.shape[0]
    cb0 = col0 // tw
    out = jax.ShapeDtypeStruct((t, 3 * WIDTH), BF16)
    return pl.pallas_call(
        _perm_kernel,
        grid=(t // SUPER, 3 * WIDTH // tw),
        in_specs=[pl.BlockSpec((SUPER, tw), lambda i, j: (i, cb0 + j))],
        out_specs=[pl.BlockSpec((SUPER, tw), lambda i, j: (i, j)),
                   pl.BlockSpec((SUPER, tw), lambda i, j: (i, j))],
        out_shape=[out, out],
        scratch_shapes=[pltpu.VMEM((tw // LANES, SUPER, LANES), F32)],
        compiler_params=_cparams("arbitrary", "arbitrary"),
    )(h1)


def _prev_block(n, dil):
    per_super = SUPER // N_BACK
    sub = SUPER // dil // N_BACK
    same_super = n % sub > 0
    return jnp.where(same_super, n - 1, n - per_super + sub - 1), jnp.logical_or(same_super, n >= per_super)


def _attn_kernel(q_ref, kp_ref, kc_ref, vp_ref, vc_ref, o_ref, lse_ref, *, dil):
    _, has_prev = _prev_block(pl.program_id(1), dil)
    a = lax.broadcasted_iota(jnp.int32, (N_BACK, 2 * N_BACK), 0)
    j = lax.broadcasted_iota(jnp.int32, (N_BACK, 2 * N_BACK), 1)
    first_key = jnp.where(has_prev, 0, N_BACK)
    ok = jnp.logical_and(j >= jnp.maximum(a, first_key), j <= a + N_BACK)
    scale = D_HEAD_DIM ** -0.5
    lse_ref[...] = jnp.zeros_like(lse_ref)
    for h in range(D_HEADS):
        sl = slice(h * D_HEAD_DIM, (h + 1) * D_HEAD_DIM)
        k = jnp.concatenate([kp_ref[:, sl], kc_ref[:, sl]], axis=0)
        v = jnp.concatenate([vp_ref[:, sl], vc_ref[:, sl]], axis=0)
        s = lax.dot_general(q_ref[:, sl], k, (((1,), (1,)), ((), ())), preferred_element_type=F32) * scale
        s = jnp.where(ok, s, -jnp.inf)
        m = jnp.max(s, axis=-1, keepdims=True)
        p = jnp.exp(s - m)
        l = jnp.sum(p, axis=-1, keepdims=True)
        o = jnp.dot(p.astype(BF16), v, preferred_element_type=F32) * (1.0 / l)
        o_ref[:, sl] = o.astype(o_ref.dtype)
        lse_ref[:, h:h + 1] = m + jnp.log(l)


def _attention(q_arr, q_cb, k_arr, k_cb, v_arr, v_cb, batch, seq, dil):
    t = batch * seq
    nb = seq // N_BACK

    def cur(cb):
        return lambda b, n: (b * nb + n, cb)

    def prev(cb):
        def index_map(b, n):
            p, valid = _prev_block(n, dil)
            return (b * nb + jnp.where(valid, p, n), cb)
        return index_map

    blk = (N_BACK, WIDTH)
    return pl.pallas_call(
        functools.partial(_attn_kernel, dil=dil),
        grid=(batch, nb),
        in_specs=[pl.BlockSpec(blk, cur(q_cb)),
                  pl.BlockSpec(blk, prev(k_cb)), pl.BlockSpec(blk, cur(k_cb)),
                  pl.BlockSpec(blk, prev(v_cb)), pl.BlockSpec(blk, cur(v_cb))],
        out_specs=[pl.BlockSpec(blk, cur(0)), pl.BlockSpec((N_BACK, LANES), cur(0))],
        out_shape=[jax.ShapeDtypeStruct((t, WIDTH), BF16), jax.ShapeDtypeStruct((t, LANES), F32)],
        compiler_params=_cparams("arbitrary", "arbitrary"),
    )(q_arr, k_arr, k_arr, v_arr, v_arr)


def _mix1_kernel(bg_ref, cg_ref, hx_ref, zc_ref, zd_ref, o1_ref, o4_ref, o16_ref, l1_ref, l4_ref, l16_ref,
                 cw_ref, o_ref, tail_ref, slab4_ref, slab16_ref, lslab_ref):
    tile = o_ref.shape[0]

    @pl.when(pl.program_id(1) == 0)
    def _():
        tail_ref[...] = jnp.zeros_like(tail_ref)

    cwid = 512
    r8 = lax.broadcasted_iota(jnp.int32, (SUBLANES, cwid), 0)
    for jb in range(WIDTH // cwid):
        sl = slice(jb * cwid, (jb + 1) * cwid)
        x = cg_ref[:, sl].astype(F32) * hx_ref[:, sl].astype(F32)
        tl = tail_ref[:, sl]
        acc = x * cw_ref[C_CONV - 1:C_CONV, sl]
        for k in range(1, C_CONV):
            xr = pltpu.roll(x, k, axis=0)
            tr = pltpu.roll(tl, k, axis=0)
            first = jnp.where(r8 < k, tr, xr[0:SUBLANES])
            xk = jnp.concatenate([first, xr[SUBLANES:]], axis=0)
            acc = acc + xk * cw_ref[C_CONV - 1 - k:C_CONV - k, sl]
        tail_ref[:, sl] = x[tile - SUBLANES:tile]
        o_ref[:, sl] = (_silu(zc_ref[:, sl].astype(F32)) * (bg_ref[:, sl].astype(F32) * acc)).astype(o_ref.dtype)

    d4, d16 = DILATIONS[1], DILATIONS[2]
    for r in range(d4):
        lslab_ref[0, pl.ds(r, tile // d4, stride=d4), :] = l4_ref[r]
    for r in range(d16):
        lslab_ref[1, pl.ds(r, tile // d16, stride=d16), :] = l16_ref[r]
    l1 = l1_ref[...]
    l4 = lslab_ref[0]
    l16 = lslab_ref[1]
    lmax = jnp.maximum(jnp.maximum(l1, l4), l16)
    e1 = jnp.exp(l1 - lmax)
    e4 = jnp.exp(l4 - lmax)
    e16 = jnp.exp(l16 - lmax)
    inv = 1.0 / (e1 + e4 + e16)
    w1, w4, w16 = e1 * inv, e4 * inv, e16 * inv
    for h in range(D_HEADS):
        sl = slice(h * D_HEAD_DIM, (h + 1) * D_HEAD_DIM)
        for r in range(d4):
            slab4_ref[pl.ds(r, tile // d4, stride=d4), :] = o4_ref[r, :, sl].astype(F32)
        for r in range(d16):
            slab16_ref[pl.ds(r, tile // d16, stride=d16), :] = o16_ref[r, :, sl].astype(F32)
        o = (w1[:, h:h + 1] * o1_ref[:, sl].astype(F32) + w4[:, h:h + 1] * slab4_ref[...]
             + w16[:, h:h + 1] * slab16_ref[...])
        o_ref[:, WIDTH + h * D_HEAD_DIM:WIDTH + (h + 1) * D_HEAD_DIM] = (
            _silu(zd_ref[:, sl].astype(F32)) * o).astype(o_ref.dtype)


def _mix1(h1, o1, o4, o16, l1, l4, l16, conv_w, batch, seq, tile=256):
    t = batch * seq
    nt = seq // tile
    per_super = SUPER // tile
    d4, d16 = DILATIONS[1], DILATIONS[2]
    n_super = t // SUPER

    def tok(cb):
        return lambda b, i: (b * nt + i, cb)

    def perm(b, i):
        g = b * nt + i
        return (g // per_super, 0, g % per_super, 0)

    def view(a, d):
        return a.reshape(n_super, d, SUPER // d, a.shape[-1])

    def pblock(d, w):
        return pl.BlockSpec((None, d, tile // d, w), perm)

    return pl.pallas_call(
        _mix1_kernel,
        grid=(batch, nt),
        in_specs=[pl.BlockSpec((tile, WIDTH), tok(0)), pl.BlockSpec((tile, WIDTH), tok(1)),
                  pl.BlockSpec((tile, WIDTH), tok(2)), pl.BlockSpec((tile, WIDTH), tok(3)),
                  pl.BlockSpec((tile, WIDTH), tok(7)),
                  pl.BlockSpec((tile, WIDTH), tok(0)), pblock(d4, WIDTH), pblock(d16, WIDTH),
                  pl.BlockSpec((tile, LANES), tok(0)), pblock(d4, LANES), pblock(d16, LANES),
                  pl.BlockSpec(conv_w.shape, lambda b, i: (0, 0))],
        out_specs=pl.BlockSpec((tile, 2 * WIDTH), tok(0)),
        out_shape=jax.ShapeDtypeStruct((t, 2 * WIDTH), BF16),
        scratch_shapes=[pltpu.VMEM((SUBLANES, WIDTH), F32),
                        pltpu.VMEM((tile, LANES), F32), pltpu.VMEM((tile, LANES), F32),
                        pltpu.VMEM((2, tile, LANES), F32)],
        compiler_params=_cparams("arbitrary", "arbitrary"),
    )(h1, h1, h1, h1, h1, o1, view(o4, d4), view(o16, d16), l1, view(l4, d4), view(l16, d16), conv_w)


def kernel(x, even_norm_g, even_w_in, gmlp_ln_g, gmlp_ln_b, gmlp_ws, gmlp_bs, ssd_conv_w, ssd_conv_b,
           ssd_dt_bias, ssd_a_log, ssd_d, ssd_norm_g, even_w_out, odd_norm_g, odd_w_in, sconv_w,
           odd_w_out, final_norm_g):
    batch, seq, d = x.shape
    assert d == D_MODEL and seq % SUPER == 0
    assert even_norm_g.shape[0] == 1 and odd_norm_g.shape[0] == 1
    t = batch * seq
    x0 = x.reshape(t, d)

    w_in0 = even_w_in.reshape(d, -1)
    w_out0 = even_w_out.reshape(-1, d)
    w_in1 = odd_w_in.reshape(d, -1)
    w_out1 = odd_w_out.reshape(-1, d)

    xn0 = _rmsnorm(x0, even_norm_g[0], BF16)
    n_main = 4 * WIDTH + B_XBC
    h0 = _matmul(xn0, w_in0, BF16, n_cols=n_main)
    w_dt = jnp.pad(w_in0[:, n_main:], ((0, 0), (0, LANES - B_HEADS)))
    dt_raw = _small_matmul(xn0, w_dt)
    y0 = _mix0(h0, dt_raw, batch, seq, gmlp_ln_g[0], gmlp_ln_b[0], gmlp_ws[0], gmlp_bs[0],
               ssd_conv_w[0], ssd_conv_b[0], ssd_dt_bias[0], ssd_a_log[0], ssd_d[0], ssd_norm_g[0])
    x1 = _matmul(y0, w_out0, F32, res=x0, tn=512)

    xn1 = _rmsnorm(x1, odd_norm_g[0], BF16)
    h1 = _matmul(xn1, w_in1, BF16)
    p4, p16 = _permute_qkv(h1, 4 * WIDTH)
    o1, l1 = _attention(h1, 4, h1, 5, h1, 6, batch, seq, DILATIONS[0])
    o4, l4 = _attention(p4, 0, p4, 1, p4, 2, batch, seq, DILATIONS[1])
    o16, l16 = _attention(p16, 0, p16, 1, p16, 2, batch, seq, DILATIONS[2])
    y1 = _mix1(h1, o1, o4, o16, l1, l4, l16, sconv_w[0], batch, seq)
    x2 = _matmul(y1, w_out1, F32, res=x1, tn=512)

    return _rmsnorm(x2, final_norm_g, F32).reshape(batch, seq, d)
```

```python
import functools

import jax
import jax.numpy as jnp
from jax import lax
from jax.experimental import pallas as pl
from jax.experimental.pallas import tpu as pltpu

F32 = jnp.float32
BF16 = jnp.bfloat16

EPS = 1e-5
D_MODEL = 2048
WIDTH = 2048
A_GROUPS = 8
CHUNK = 128
B_HEAD_DIM = 64
B_HEADS = WIDTH // B_HEAD_DIM
B_GROUPS = 8
B_STATE = 128
B_CONV = 4
B_XBC = WIDTH + 2 * B_GROUPS * B_STATE
HEADS_PER_GROUP = B_HEADS // B_GROUPS
GROUP_W = WIDTH // B_GROUPS
C_CONV = 3
D_HEAD_DIM = 128
D_HEADS = WIDTH // D_HEAD_DIM
N_BACK = 128
DILATIONS = (1, 4, 16)
SUPER = N_BACK * DILATIONS[-1]
LANES = 128
SUBLANES = 8
VMEM_LIMIT = 56 * 1024 * 1024


def _cparams(*sem):
    return pltpu.CompilerParams(dimension_semantics=sem, vmem_limit_bytes=VMEM_LIMIT)


def _silu(z):
    hz = 0.5 * z
    return hz * jnp.tanh(hz) + hz


def _rmsnorm_kernel(x_ref, g_ref, o_ref):
    x = x_ref[...]
    ms = jnp.mean(x * x, axis=-1, keepdims=True)
    o_ref[...] = (x * lax.rsqrt(ms + EPS) * g_ref[...]).astype(o_ref.dtype)


def _rmsnorm(x2d, g, out_dtype, tm=512):
    t, d = x2d.shape
    return pl.pallas_call(
        _rmsnorm_kernel,
        grid=(t // tm,),
        in_specs=[pl.BlockSpec((tm, d), lambda i: (i, 0)),
                  pl.BlockSpec((1, d), lambda i: (0, 0))],
        out_specs=pl.BlockSpec((tm, d), lambda i: (i, 0)),
        out_shape=jax.ShapeDtypeStruct((t, d), out_dtype),
        compiler_params=_cparams("arbitrary"),
    )(x2d, g.reshape(1, d))


def _matmul_kernel(*refs, n_lhs, has_res, cast_rows):
    a_refs = refs[:n_lhs]
    w_ref = refs[n_lhs]
    if has_res:
        r_ref, o_ref, wb_ref = refs[n_lhs + 1:]
    else:
        o_ref, wb_ref = refs[n_lhs + 1:]

    @pl.when(pl.program_id(1) == 0)
    def _():
        def body(k, carry):
            r = pl.multiple_of(k * cast_rows, cast_rows)
            wb_ref[pl.ds(r, cast_rows), :] = w_ref[pl.ds(r, cast_rows), :].astype(BF16)
            return carry
        lax.fori_loop(0, w_ref.shape[0] // cast_rows, body, 0)

    acc = None
    k0 = 0
    for a_ref in a_refs:
        kw = a_ref.shape[1]
        part = jnp.dot(a_ref[...], wb_ref[k0:k0 + kw, :], preferred_element_type=F32)
        acc = part if acc is None else acc + part
        k0 += kw
    if has_res:
        acc = acc + r_ref[...]
    o_ref[...] = acc.astype(o_ref.dtype)


def _matmul(lhs, w, out_dtype, *, n_cols=None, res=None, tm=1024, tn=1024):
    m = lhs[0].shape[0]
    k = w.shape[0]
    n = w.shape[1] if n_cols is None else n_cols
    in_specs = [pl.BlockSpec((tm, a.shape[1]), lambda j, i: (i, 0)) for a in lhs]
    in_specs.append(pl.BlockSpec((k, tn), lambda j, i: (0, j)))
    args = list(lhs) + [w]
    if res is not None:
        in_specs.append(pl.BlockSpec((tm, tn), lambda j, i: (i, j)))
        args.append(res)
    return pl.pallas_call(
        functools.partial(_matmul_kernel, n_lhs=len(lhs), has_res=res is not None, cast_rows=256),
        grid=(n // tn, m // tm),
        in_specs=in_specs,
        out_specs=pl.BlockSpec((tm, tn), lambda j, i: (i, j)),
        out_shape=jax.ShapeDtypeStruct((m, n), out_dtype),
        scratch_shapes=[pltpu.VMEM((k, tn), BF16)],
        compiler_params=_cparams("arbitrary", "arbitrary"),
    )(*args)


def _small_matmul_kernel(a_ref, w_ref, o_ref):
    o_ref[...] = jnp.dot(a_ref[...], w_ref[...].astype(BF16), preferred_element_type=F32)


def _small_matmul(a, w, tm=1024):
    m, k = a.shape
    n = w.shape[1]
    return pl.pallas_call(
        _small_matmul_kernel,
        grid=(m // tm,),
        in_specs=[pl.BlockSpec((tm, k), lambda i: (i, 0)),
                  pl.BlockSpec((k, n), lambda i: (0, 0))],
        out_specs=pl.BlockSpec((tm, n), lambda i: (i, 0)),
        out_shape=jax.ShapeDtypeStruct((m, n), F32),
        compiler_params=_cparams("arbitrary"),
    )(a, w)


def _softplus(x):
    return jnp.maximum(x, 0.0) + jnp.log1p(jnp.exp(-jnp.abs(x)))


def _split3(x):
    hi = x.astype(BF16)
    r1 = x - hi.astype(F32)
    mid = r1.astype(BF16)
    lo = (r1 - mid.astype(F32)).astype(BF16)
    return hi, mid, lo


def _mix0_kernel(u_ref, v_ref, za_ref, zb_ref, xbc_ref, dt_ref,
                 lng_ref, lnb_ref, ws_ref, bst_ref, cw_ref, cbias_ref, dtb_ref, alog_ref, dexp_ref, ng_ref,
                 o_ref,
                 wsb_ref, xbuf_ref, xs_ref, bm_ref, cm_ref, y_ref, st_ref):
    q = CHUNK
    row = lax.broadcasted_iota(jnp.int32, (q, q), 0)
    col = lax.broadcasted_iota(jnp.int32, (q, q), 1)
    causal = col <= row

    @pl.when(pl.program_id(1) == 0)
    def _():
        xbuf_ref[0:SUBLANES, :] = jnp.zeros((SUBLANES, B_XBC), F32)
        st_ref[...] = jnp.zeros_like(st_ref)
        for g in range(A_GROUPS):
            wsb_ref[g] = jnp.where(causal, ws_ref[g], 0.0).astype(BF16)

    v = v_ref[...].astype(F32)
    mu = jnp.mean(v, axis=-1, keepdims=True)
    xc = v - mu
    var = jnp.mean(xc * xc, axis=-1, keepdims=True)
    rstd = lax.rsqrt(var + EPS)
    for g in range(A_GROUPS):
        sl = slice(g * GROUP_W, (g + 1) * GROUP_W)
        vg = v_ref[:, sl].astype(F32)
        vn = ((vg - mu) * rstd * lng_ref[:, sl] + lnb_ref[:, sl]).astype(BF16)
        mixed = jnp.dot(wsb_ref[g], vn, preferred_element_type=F32) + bst_ref[:, g:g + 1]
        z = za_ref[:, sl].astype(F32)
        u = u_ref[:, sl].astype(F32)
        o_ref[:, sl] = (_silu(z) * (u * mixed)).astype(o_ref.dtype)

    cwid = 256
    for j in range(B_XBC // cwid):
        sl = slice(j * cwid, (j + 1) * cwid)
        xbuf_ref[SUBLANES:, sl] = xbc_ref[:, sl].astype(F32)
        acc = cbias_ref[:, sl]
        for k in range(B_CONV):
            lo = SUBLANES - (B_CONV - 1) + k
            acc = acc + xbuf_ref[lo:lo + q, sl] * cw_ref[k:k + 1, sl]
        xbuf_ref[0:SUBLANES, sl] = xbuf_ref[q:q + SUBLANES, sl]
        act = _silu(acc)
        lo = j * cwid
        if lo < WIDTH:
            xs_ref[:, lo:lo + cwid] = act
        elif lo < WIDTH + B_GROUPS * B_STATE:
            bm_ref[:, lo - WIDTH:lo - WIDTH + cwid] = act
        else:
            off = lo - WIDTH - B_GROUPS * B_STATE
            cm_ref[:, off:off + cwid] = act

    dt = _softplus(dt_ref[...] + dtb_ref[...])
    adt = dt * (-jnp.exp(alog_ref[...]))
    tril = jnp.where(causal, 1.0, 0.0).astype(BF16)
    a_cs = sum(jnp.dot(tril, part, preferred_element_type=F32) for part in _split3(adt))
    a_cs_t = a_cs.T
    dt_t = dt.T

    for g in range(B_GROUPS):
        gs = slice(g * B_STATE, (g + 1) * B_STATE)
        bg = bm_ref[:, gs]
        cgb = cm_ref[:, gs].astype(BF16)
        bg_t = bg.T
        cb = lax.dot_general(cgb, bg.astype(BF16), (((1,), (1,)), ((), ())),
                             preferred_element_type=F32)
        for r in range(HEADS_PER_GROUP):
            h = g * HEADS_PER_GROUP + r
            hs = slice(h * B_HEAD_DIM, (h + 1) * B_HEAD_DIM)
            colv = a_cs[:, h:h + 1]
            rowv = a_cs_t[h:h + 1, :]
            dtr = dt_t[h:h + 1, :]
            decay = jnp.exp(jnp.where(causal, colv - rowv, -jnp.inf))
            mh = (cb * decay * dtr).astype(BF16)
            xh = xs_ref[:, hs]
            xhb = xh.astype(BF16)
            state = st_ref[h]
            y = jnp.dot(mh, xhb, preferred_element_type=F32)
            y = y + jnp.exp(colv) * jnp.dot(cgb, state.astype(BF16), preferred_element_type=F32)
            y_ref[:, hs] = y + dexp_ref[:, hs] * xh
            last = a_cs_t[h:h + 1, q - 1:q]
            wrow = jnp.exp(last - rowv) * dtr
            new = jnp.dot((bg_t * wrow).astype(BF16), xhb, preferred_element_type=F32)
            st_ref[h] = state * jnp.exp(last) + new

    for g in range(B_GROUPS):
        sl = slice(g * GROUP_W, (g + 1) * GROUP_W)
        y = y_ref[:, sl] * _silu(zb_ref[:, sl].astype(F32))
        ms = jnp.mean(y * y, axis=-1, keepdims=True)
        o_ref[:, WIDTH + g * GROUP_W:WIDTH + (g + 1) * GROUP_W] = (
            y * lax.rsqrt(ms + EPS) * ng_ref[:, sl]).astype(o_ref.dtype)


def _mix0(h0, dt_raw, batch, seq, lng, lnb, ws, bs, conv_w, conv_b, dt_bias, a_log, d_skip, norm_g):
    t = batch * seq
    nc = seq // CHUNK
    pad = LANES - B_HEADS
    row = lambda a: a.reshape(1, -1)
    params = [row(lng), row(lnb), ws, bs.T, conv_w, row(conv_b),
              row(jnp.pad(dt_bias, (0, pad))), row(jnp.pad(a_log, (0, pad))),
              row(jnp.repeat(d_skip, B_HEAD_DIM)), row(norm_g)]
    tok = lambda cb: (lambda b, c: (b * nc + c, cb))
    full = lambda a: pl.BlockSpec(a.shape, lambda b, c: (0,) * a.ndim)
    in_specs = [pl.BlockSpec((CHUNK, WIDTH), tok(0)),
                pl.BlockSpec((CHUNK, WIDTH), tok(1)),
                pl.BlockSpec((CHUNK, WIDTH), tok(2)),
                pl.BlockSpec((CHUNK, WIDTH), tok(3)),
                pl.BlockSpec((CHUNK, B_XBC), tok(2)),
                pl.BlockSpec((CHUNK, LANES), tok(0))]
    in_specs += [full(p) for p in params]
    return pl.pallas_call(
        _mix0_kernel,
        grid=(batch, nc),
        in_specs=in_specs,
        out_specs=pl.BlockSpec((CHUNK, 2 * WIDTH), tok(0)),
        out_shape=jax.ShapeDtypeStruct((t, 2 * WIDTH), BF16),
        scratch_shapes=[pltpu.VMEM((A_GROUPS, CHUNK, CHUNK), BF16),
                        pltpu.VMEM((SUBLANES + CHUNK, B_XBC), F32),
                        pltpu.VMEM((CHUNK, WIDTH), F32),
                        pltpu.VMEM((CHUNK, B_GROUPS * B_STATE), F32),
                        pltpu.VMEM((CHUNK, B_GROUPS * B_STATE), F32),
                        pltpu.VMEM((CHUNK, WIDTH), F32),
                        pltpu.VMEM((B_HEADS, B_STATE, B_HEAD_DIM), F32)],
        compiler_params=_cparams("arbitrary", "arbitrary"),
    )(h0, h0, h0, h0, h0, dt_raw, *params)


def _shortconv_kernel(bg_ref, cg_ref, hx_ref, zc_ref, cw_ref, o_ref, xbuf_ref):
    tile = o_ref.shape[0]

    @pl.when(pl.program_id(1) == 0)
    def _():
        xbuf_ref[0:SUBLANES, :] = jnp.zeros((SUBLANES, WIDTH), F32)

    cwid, rows = 256, 128
    for j in range(WIDTH // cwid):
        sl = slice(j * cwid, (j + 1) * cwid)
        for i in range(tile // rows):
            r0 = i * rows
            xbuf_ref[SUBLANES + r0:SUBLANES + r0 + rows, sl] = (
                cg_ref[r0:r0 + rows, sl].astype(F32) * hx_ref[r0:r0 + rows, sl].astype(F32))
        for i in range(tile // rows):
            r0 = i * rows
            acc = None
            for k in range(C_CONV):
                lo = SUBLANES - (C_CONV - 1) + k + r0
                term = xbuf_ref[lo:lo + rows, sl] * cw_ref[k:k + 1, sl]
                acc = term if acc is None else acc + term
            gate = _silu(zc_ref[r0:r0 + rows, sl].astype(F32)) * bg_ref[r0:r0 + rows, sl].astype(F32)
            o_ref[r0:r0 + rows, sl] = (gate * acc).astype(o_ref.dtype)
        xbuf_ref[0:SUBLANES, sl] = xbuf_ref[tile:tile + SUBLANES, sl]


def _shortconv(h1, conv_w, batch, seq, tile=512):
    t = batch * seq
    nt = seq // tile
    tok = lambda cb: (lambda b, i: (b * nt + i, cb))
    return pl.pallas_call(
        _shortconv_kernel,
        grid=(batch, nt),
        in_specs=[pl.BlockSpec((tile, WIDTH), tok(0)), pl.BlockSpec((tile, WIDTH), tok(1)),
                  pl.BlockSpec((tile, WIDTH), tok(2)), pl.BlockSpec((tile, WIDTH), tok(3)),
                  pl.BlockSpec(conv_w.shape, lambda b, i: (0, 0))],
        out_specs=pl.BlockSpec((tile, WIDTH), tok(0)),
        out_shape=jax.ShapeDtypeStruct((t, WIDTH), BF16),
        scratch_shapes=[pltpu.VMEM((SUBLANES + tile, WIDTH), F32)],
        compiler_params=_cparams("arbitrary", "arbitrary"),
    )(h1, h1, h1, h1, conv_w)


def _attn_kernel(q_ref, kp_ref, kc_ref, vp_ref, vc_ref, zd_ref, o_ref,
                 qs_ref, ks_ref, vs_ref, os_ref, ls_ref, bias_ref):
    first_super = pl.program_id(2) == 0
    nq = N_BACK
    a = lax.broadcasted_iota(jnp.int32, (nq, 2 * nq), 0)
    j = lax.broadcasted_iota(jnp.int32, (nq, 2 * nq), 1)
    in_band = jnp.logical_and(j >= a, j <= a + nq)
    bias_ref[0] = jnp.where(in_band, 0.0, -jnp.inf)
    bias_ref[1] = jnp.where(jnp.logical_and(in_band, j >= nq), 0.0, -jnp.inf)
    scale = D_HEAD_DIM ** -0.5
    blocks = SUPER // nq

    for hh in range(q_ref.shape[1] // D_HEAD_DIM):
        cs = slice(hh * D_HEAD_DIM, (hh + 1) * D_HEAD_DIM)
        qs_ref[...] = q_ref[:, cs].astype(F32) * scale
        ks_ref[0:SUPER] = kp_ref[:, cs].astype(F32)
        ks_ref[SUPER:] = kc_ref[:, cs].astype(F32)
        vs_ref[0:SUPER] = vp_ref[:, cs].astype(F32)
        vs_ref[SUPER:] = vc_ref[:, cs].astype(F32)

        for pi, d in enumerate(DILATIONS):
            per_res = SUPER // d // nq

            def block(i, carry, pi=pi, d=d, per_res=per_res):
                res, sub = i // per_res, i % per_res
                q0 = res + sub * (nq * d)
                k0 = SUPER + q0 - nq * d
                if d == 1:
                    q0 = pl.multiple_of(q0, nq)
                    k0 = pl.multiple_of(k0, nq)
                    qsl, ksl = pl.ds(q0, nq), pl.ds(k0, 2 * nq)
                else:
                    qsl, ksl = pl.ds(q0, nq, stride=d), pl.ds(k0, 2 * nq, stride=d)
                no_prev = jnp.logical_and(first_super, sub == 0)
                qb = qs_ref[qsl, :].astype(BF16)
                kb = ks_ref[ksl, :].astype(BF16)
                vb = vs_ref[ksl, :].astype(BF16)
                s = lax.dot_general(qb, kb, (((1,), (1,)), ((), ())), preferred_element_type=F32)
                s = s + bias_ref[no_prev.astype(jnp.int32)]
                m = jnp.max(s, axis=-1, keepdims=True)
                p = jnp.exp(s - m)
                l = jnp.sum(p, axis=-1, keepdims=True)
                o = jnp.dot(p.astype(BF16), vb, preferred_element_type=F32) * (1.0 / l)
                os_ref[pi, qsl, :] = o
                ls_ref[pi, qsl, :] = jnp.broadcast_to(m + jnp.log(l), (nq, LANES))
                return carry

            lax.fori_loop(0, blocks, block, 0, unroll=2)

        rows = 256

        def combine(c, carry, cs=cs):
            rs = pl.ds(pl.multiple_of(c * rows, rows), rows)
            l1, l4, l16 = ls_ref[0, rs, :], ls_ref[1, rs, :], ls_ref[2, rs, :]
            mx = jnp.maximum(jnp.maximum(l1, l4), l16)
            e1, e4, e16 = jnp.exp(l1 - mx), jnp.exp(l4 - mx), jnp.exp(l16 - mx)
            o = (e1 * os_ref[0, rs, :] + e4 * os_ref[1, rs, :] + e16 * os_ref[2, rs, :]) * (1.0 / (e1 + e4 + e16))
            o_ref[rs, cs] = (_silu(zd_ref[rs, cs].astype(F32)) * o).astype(o_ref.dtype)
            return carry

        lax.fori_loop(0, SUPER // rows, combine, 0)


def _attention(h1, col0, batch, seq, hw=256):
    t = batch * seq
    ns = seq // SUPER
    cb0 = col0 // hw
    per = WIDTH // hw

    def cur(part):
        return lambda b, g, s: (b * ns + s, cb0 + part * per + g)

    def prev(part):
        return lambda b, g, s: (b * ns + jnp.maximum(s - 1, 0), cb0 + part * per + g)

    blk = (SUPER, hw)
    return pl.pallas_call(
        _attn_kernel,
        grid=(batch, per, ns),
        in_specs=[pl.BlockSpec(blk, cur(0)),
                  pl.BlockSpec(blk, prev(1)), pl.BlockSpec(blk, cur(1)),
                  pl.BlockSpec(blk, prev(2)), pl.BlockSpec(blk, cur(2)),
                  pl.BlockSpec(blk, cur(3))],
        out_specs=pl.BlockSpec(blk, lambda b, g, s: (b * ns + s, g)),
        out_shape=jax.ShapeDtypeStruct((t, WIDTH), BF16),
        scratch_shapes=[pltpu.VMEM((SUPER, LANES), F32),
                        pltpu.VMEM((2 * SUPER, LANES), F32),
                        pltpu.VMEM((2 * SUPER, LANES), F32),
                        pltpu.VMEM((len(DILATIONS), SUPER, LANES), F32),
                        pltpu.VMEM((len(DILATIONS), SUPER, LANES), F32),
                        pltpu.VMEM((2, N_BACK, 2 * N_BACK), F32)],
        compiler_params=_cparams("arbitrary", "arbitrary", "arbitrary"),
    )(h1, h1, h1, h1, h1, h1)


def kernel(x, even_norm_g, even_w_in, gmlp_ln_g, gmlp_ln_b, gmlp_ws, gmlp_bs, ssd_conv_w, ssd_conv_b,
           ssd_dt_bias, ssd_a_log, ssd_d, ssd_norm_g, even_w_out, odd_norm_g, odd_w_in, sconv_w,
           odd_w_out, final_norm_g):
    batch, seq, d = x.shape
    assert d == D_MODEL and seq % SUPER == 0
    assert even_norm_g.shape[0] == 1 and odd_norm_g.shape[0] == 1
    t = batch * seq
    x0 = x.reshape(t, d)
    w_in0 = even_w_in.reshape(d, -1)
    w_out0 = even_w_out.reshape(-1, d)
    w_in1 = odd_w_in.reshape(d, -1)
    w_out1 = odd_w_out.reshape(-1, d)

    xn0 = _rmsnorm(x0, even_norm_g[0], BF16)
    n_main = 4 * WIDTH + B_XBC
    h0 = _matmul([xn0], w_in0, BF16, n_cols=n_main)
    w_dt = jnp.pad(w_in0[:, n_main:], ((0, 0), (0, LANES - B_HEADS)))
    dt_raw = _small_matmul(xn0, w_dt)
    y0 = _mix0(h0, dt_raw, batch, seq, gmlp_ln_g[0], gmlp_ln_b[0], gmlp_ws[0], gmlp_bs[0],
               ssd_conv_w[0], ssd_conv_b[0], ssd_dt_bias[0], ssd_a_log[0], ssd_d[0], ssd_norm_g[0])
    x1 = _matmul([y0], w_out0, F32, res=x0, tn=512)

    xn1 = _rmsnorm(x1, odd_norm_g[0], BF16)
    h1 = _matmul([xn1], w_in1, BF16)
    yc = _shortconv(h1, sconv_w[0], batch, seq)
    yd = _attention(h1, 4 * WIDTH, batch, seq)
    x2 = _matmul([yc, yd], w_out1, F32, res=x1, tn=512)

    return _rmsnorm(x2, final_norm_g, F32).reshape(batch, seq, d)
```

```python
import functools

import jax
import jax.numpy as jnp
from jax import lax
from jax.experimental import pallas as pl
from jax.experimental.pallas import tpu as pltpu

F32 = jnp.float32
BF16 = jnp.bfloat16

EPS = 1e-5
D_MODEL = 2048
WIDTH = 2048
A_GROUPS = 8
CHUNK = 128
B_HEAD_DIM = 64
B_HEADS = WIDTH // B_HEAD_DIM
B_GROUPS = 8
B_STATE = 128
B_CONV = 4
B_XBC = WIDTH + 2 * B_GROUPS * B_STATE
HEADS_PER_GROUP = B_HEADS // B_GROUPS
GROUP_W = WIDTH // B_GROUPS
C_CONV = 3
D_HEAD_DIM = 128
D_HEADS = WIDTH // D_HEAD_DIM
N_BACK = 128
DILATIONS = (1, 4, 16)
SUPER = N_BACK * DILATIONS[-1]
LANES = 128
SUBLANES = 8
VMEM_LIMIT = 56 * 1024 * 1024


def _cparams(*sem):
    return pltpu.CompilerParams(dimension_semantics=sem, vmem_limit_bytes=VMEM_LIMIT)


def _silu(z):
    hz = 0.5 * z
    return hz * jnp.tanh(hz) + hz


def _rmsnorm_kernel(x_ref, g_ref, o_ref):
    x = x_ref[...]
    ms = jnp.mean(x * x, axis=-1, keepdims=True)
    o_ref[...] = (x * lax.rsqrt(ms + EPS) * g_ref[...]).astype(o_ref.dtype)


def _rmsnorm(x2d, g, out_dtype, tm=512):
    t, d = x2d.shape
    return pl.pallas_call(
        _rmsnorm_kernel,
        grid=(t // tm,),
        in_specs=[pl.BlockSpec((tm, d), lambda i: (i, 0)),
                  pl.BlockSpec((1, d), lambda i: (0, 0))],
        out_specs=pl.BlockSpec((tm, d), lambda i: (i, 0)),
        out_shape=jax.ShapeDtypeStruct((t, d), out_dtype),
        compiler_params=_cparams("arbitrary"),
    )(x2d, g.reshape(1, d))


def _matmul_kernel(*refs, n_lhs, has_res, cast_rows):
    a_refs = refs[:n_lhs]
    w_ref = refs[n_lhs]
    if has_res:
        r_ref, o_ref, wb_ref = refs[n_lhs + 1:]
    else:
        o_ref, wb_ref = refs[n_lhs + 1:]

    @pl.when(pl.program_id(1) == 0)
    def _():
        def body(k, carry):
            r = pl.multiple_of(k * cast_rows, cast_rows)
            wb_ref[pl.ds(r, cast_rows), :] = w_ref[pl.ds(r, cast_rows), :].astype(BF16)
            return carry
        lax.fori_loop(0, w_ref.shape[0] // cast_rows, body, 0)

    acc = None
    k0 = 0
    for a_ref in a_refs:
        kw = a_ref.shape[1]
        part = jnp.dot(a_ref[...], wb_ref[k0:k0 + kw, :], preferred_element_type=F32)
        acc = part if acc is None else acc + part
        k0 += kw
    if has_res:
        acc = acc + r_ref[...]
    o_ref[...] = acc.astype(o_ref.dtype)


def _matmul(lhs, w, out_dtype, *, n_cols=None, res=None, tm=1024, tn=1024):
    m = lhs[0].shape[0]
    k = w.shape[0]
    n = w.shape[1] if n_cols is None else n_cols
    in_specs = [pl.BlockSpec((tm, a.shape[1]), lambda j, i: (i, 0)) for a in lhs]
    in_specs.append(pl.BlockSpec((k, tn), lambda j, i: (0, j)))
    args = list(lhs) + [w]
    if res is not None:
        in_specs.append(pl.BlockSpec((tm, tn), lambda j, i: (i, j)))
        args.append(res)
    return pl.pallas_call(
        functools.partial(_matmul_kernel, n_lhs=len(lhs), has_res=res is not None, cast_rows=256),
        grid=(n // tn, m // tm),
        in_specs=in_specs,
        out_specs=pl.BlockSpec((tm, tn), lambda j, i: (i, j)),
        out_shape=jax.ShapeDtypeStruct((m, n), out_dtype),
        scratch_shapes=[pltpu.VMEM((k, tn), BF16)],
        compiler_params=_cparams("arbitrary", "arbitrary"),
    )(*args)


def _small_matmul_kernel(a_ref, w_ref, o_ref):
    o_ref[...] = jnp.dot(a_ref[...], w_ref[...].astype(BF16), preferred_element_type=F32)


def _small_matmul(a, w, tm=1024):
    m, k = a.shape
    n = w.shape[1]
    return pl.pallas_call(
        _small_matmul_kernel,
        grid=(m // tm,),
        in_specs=[pl.BlockSpec((tm, k), lambda i: (i, 0)),
                  pl.BlockSpec((k, n), lambda i: (0, 0))],
        out_specs=pl.BlockSpec((tm, n), lambda i: (i, 0)),
        out_shape=jax.ShapeDtypeStruct((m, n), F32),
        compiler_params=_cparams("arbitrary"),
    )(a, w)


def _softplus(x):
    return jnp.maximum(x, 0.0) + jnp.log1p(jnp.exp(-jnp.abs(x)))


def _split3(x):
    hi = x.astype(BF16)
    r1 = x - hi.astype(F32)
    mid = r1.astype(BF16)
    lo = (r1 - mid.astype(F32)).astype(BF16)
    return hi, mid, lo


def _mix0_kernel(u_ref, v_ref, za_ref, zb_ref, xbc_ref, dt_ref, xres_ref, wout_ref, g1_ref,
                 lng_ref, lnb_ref, ws_ref, bst_ref, cw_ref, cbias_ref, dtb_ref, alog_ref, dexp_ref, ng_ref,
                 x1_ref, xn1_ref,
                 o_ref, yprev_ref, wsb_ref, xbuf_ref, xs_ref, bm_ref, cm_ref, y_ref, st_ref):
    q = CHUNK
    row = lax.broadcasted_iota(jnp.int32, (q, q), 0)
    col = lax.broadcasted_iota(jnp.int32, (q, q), 1)
    causal = col <= row

    @pl.when(pl.program_id(1) == 0)
    def _():
        o_ref[...] = jnp.zeros_like(o_ref)
        xbuf_ref[0:SUBLANES, :] = jnp.zeros((SUBLANES, B_XBC), F32)
        st_ref[...] = jnp.zeros_like(st_ref)
        for g in range(A_GROUPS):
            wsb_ref[g] = jnp.where(causal, ws_ref[g], 0.0).astype(BF16)

    yprev_ref[...] = o_ref[...]
    ncol = 512
    ss = jnp.zeros((q, 1), F32)
    for nb in range(D_MODEL // ncol):
        cs = slice(nb * ncol, (nb + 1) * ncol)
        xc = jnp.dot(yprev_ref[...], wout_ref[:, cs], preferred_element_type=F32) + xres_ref[:, cs]
        x1_ref[:, cs] = xc
        ss = ss + jnp.sum(xc * xc, axis=-1, keepdims=True)
    rstd1 = lax.rsqrt(ss * (1.0 / D_MODEL) + EPS)
    for nb in range(D_MODEL // ncol):
        cs = slice(nb * ncol, (nb + 1) * ncol)
        xn1_ref[:, cs] = (x1_ref[:, cs] * rstd1 * g1_ref[:, cs]).astype(xn1_ref.dtype)

    v = v_ref[...].astype(F32)
    mu = jnp.mean(v, axis=-1, keepdims=True)
    xc = v - mu
    var = jnp.mean(xc * xc, axis=-1, keepdims=True)
    rstd = lax.rsqrt(var + EPS)
    for g in range(A_GROUPS):
        sl = slice(g * GROUP_W, (g + 1) * GROUP_W)
        vg = v_ref[:, sl].astype(F32)
        vn = ((vg - mu) * rstd * lng_ref[:, sl] + lnb_ref[:, sl]).astype(BF16)
        mixed = jnp.dot(wsb_ref[g], vn, preferred_element_type=F32) + bst_ref[:, g:g + 1]
        z = za_ref[:, sl].astype(F32)
        u = u_ref[:, sl].astype(F32)
        o_ref[:, sl] = (_silu(z) * (u * mixed)).astype(o_ref.dtype)

    cwid = 256
    for j in range(B_XBC // cwid):
        sl = slice(j * cwid, (j + 1) * cwid)
        xbuf_ref[SUBLANES:, sl] = xbc_ref[:, sl].astype(F32)
        acc = cbias_ref[:, sl]
        for k in range(B_CONV):
            lo = SUBLANES - (B_CONV - 1) + k
            acc = acc + xbuf_ref[lo:lo + q, sl] * cw_ref[k:k + 1, sl]
        xbuf_ref[0:SUBLANES, sl] = xbuf_ref[q:q + SUBLANES, sl]
        act = _silu(acc)
        lo = j * cwid
        if lo < WIDTH:
            xs_ref[:, lo:lo + cwid] = act
        elif lo < WIDTH + B_GROUPS * B_STATE:
            bm_ref[:, lo - WIDTH:lo - WIDTH + cwid] = act
        else:
            off = lo - WIDTH - B_GROUPS * B_STATE
            cm_ref[:, off:off + cwid] = act

    dt = _softplus(dt_ref[...] + dtb_ref[...])
    adt = dt * (-jnp.exp(alog_ref[...]))
    tril = jnp.where(causal, 1.0, 0.0).astype(BF16)
    a_cs = sum(jnp.dot(tril, part, preferred_element_type=F32) for part in _split3(adt))
    a_cs_t = a_cs.T
    dt_t = dt.T

    for g in range(B_GROUPS):
        gs = slice(g * B_STATE, (g + 1) * B_STATE)
        bg = bm_ref[:, gs]
        cgb = cm_ref[:, gs].astype(BF16)
        bg_t = bg.T
        cb = lax.dot_general(cgb, bg.astype(BF16), (((1,), (1,)), ((), ())),
                             preferred_element_type=F32)
        for r in range(HEADS_PER_GROUP):
            h = g * HEADS_PER_GROUP + r
            hs = slice(h * B_HEAD_DIM, (h + 1) * B_HEAD_DIM)
            colv = a_cs[:, h:h + 1]
            rowv = a_cs_t[h:h + 1, :]
            dtr = dt_t[h:h + 1, :]
            decay = jnp.exp(jnp.where(causal, colv - rowv, -jnp.inf))
            mh = (cb * decay * dtr).astype(BF16)
            xh = xs_ref[:, hs]
            xhb = xh.astype(BF16)
            state = st_ref[h]
            y = jnp.dot(mh, xhb, preferred_element_type=F32)
            y = y + jnp.exp(colv) * jnp.dot(cgb, state.astype(BF16), preferred_element_type=F32)
            y_ref[:, hs] = y + dexp_ref[:, hs] * xh
            last = a_cs_t[h:h + 1, q - 1:q]
            wrow = jnp.exp(last - rowv) * dtr
            new = jnp.dot((bg_t * wrow).astype(BF16), xhb, preferred_element_type=F32)
            st_ref[h] = state * jnp.exp(last) + new

    for g in range(B_GROUPS):
        sl = slice(g * GROUP_W, (g + 1) * GROUP_W)
        y = y_ref[:, sl] * _silu(zb_ref[:, sl].astype(F32))
        ms = jnp.mean(y * y, axis=-1, keepdims=True)
        o_ref[:, WIDTH + g * GROUP_W:WIDTH + (g + 1) * GROUP_W] = (
            y * lax.rsqrt(ms + EPS) * ng_ref[:, sl]).astype(o_ref.dtype)


def _mix0(h0, dt_raw, x0, w_out, next_norm_g, batch, seq,
          lng, lnb, ws, bs, conv_w, conv_b, dt_bias, a_log, d_skip, norm_g):
    t = batch * seq
    nc = seq // CHUNK
    pad = LANES - B_HEADS
    row = lambda a: a.reshape(1, -1)
    params = [row(lng), row(lnb), ws, bs.T, conv_w, row(conv_b),
              row(jnp.pad(dt_bias, (0, pad))), row(jnp.pad(a_log, (0, pad))),
              row(jnp.repeat(d_skip, B_HEAD_DIM)), row(norm_g)]
    tok = lambda cb: (lambda b, c: (b * nc + jnp.minimum(c, nc - 1), cb))
    lag = lambda b, c: (b * nc + jnp.maximum(c - 1, 0), 0)
    full = lambda a: pl.BlockSpec(a.shape, lambda b, c: (0,) * a.ndim)
    in_specs = [pl.BlockSpec((CHUNK, WIDTH), tok(0)),
                pl.BlockSpec((CHUNK, WIDTH), tok(1)),
                pl.BlockSpec((CHUNK, WIDTH), tok(2)),
                pl.BlockSpec((CHUNK, WIDTH), tok(3)),
                pl.BlockSpec((CHUNK, B_XBC), tok(2)),
                pl.BlockSpec((CHUNK, LANES), tok(0)),
                pl.BlockSpec((CHUNK, D_MODEL), lag),
                pl.BlockSpec(w_out.shape, lambda b, c: (0, 0), pipeline_mode=pl.Buffered(1)),
                full(row(next_norm_g))]
    in_specs += [full(p) for p in params]
    return pl.pallas_call(
        _mix0_kernel,
        grid=(batch, nc + 1),
        in_specs=in_specs,
        out_specs=[pl.BlockSpec((CHUNK, D_MODEL), lag), pl.BlockSpec((CHUNK, D_MODEL), lag)],
        out_shape=[jax.ShapeDtypeStruct((t, D_MODEL), F32), jax.ShapeDtypeStruct((t, D_MODEL), BF16)],
        scratch_shapes=[pltpu.VMEM((CHUNK, 2 * WIDTH), BF16),
                        pltpu.VMEM((CHUNK, 2 * WIDTH), BF16),
                        pltpu.VMEM((A_GROUPS, CHUNK, CHUNK), BF16),
                        pltpu.VMEM((SUBLANES + CHUNK, B_XBC), F32),
                        pltpu.VMEM((CHUNK, WIDTH), F32),
                        pltpu.VMEM((CHUNK, B_GROUPS * B_STATE), F32),
                        pltpu.VMEM((CHUNK, B_GROUPS * B_STATE), F32),
                        pltpu.VMEM((CHUNK, WIDTH), F32),
                        pltpu.VMEM((B_HEADS, B_STATE, B_HEAD_DIM), F32)],
        compiler_params=_cparams("arbitrary", "arbitrary"),
    )(h0, h0, h0, h0, h0, dt_raw, x0, w_out, row(next_norm_g), *params)


def _cast_kernel(w_ref, o_ref):
    o_ref[...] = w_ref[...].astype(o_ref.dtype)


def _cast_bf16(w, tr=512):
    r, c = w.shape
    return pl.pallas_call(
        _cast_kernel,
        grid=(r // tr,),
        in_specs=[pl.BlockSpec((tr, c), lambda i: (i, 0))],
        out_specs=pl.BlockSpec((tr, c), lambda i: (i, 0)),
        out_shape=jax.ShapeDtypeStruct((r, c), BF16),
        compiler_params=_cparams("arbitrary"),
    )(w)


def _shortconv_kernel(bg_ref, cg_ref, hx_ref, zc_ref, cw_ref, o_ref, xbuf_ref):
    tile = o_ref.shape[0]

    @pl.when(pl.program_id(1) == 0)
    def _():
        xbuf_ref[0:SUBLANES, :] = jnp.zeros((SUBLANES, WIDTH), F32)

    cwid, rows = 256, 128
    for j in range(WIDTH // cwid):
        sl = slice(j * cwid, (j + 1) * cwid)
        for i in range(tile // rows):
            r0 = i * rows
            xbuf_ref[SUBLANES + r0:SUBLANES + r0 + rows, sl] = (
                cg_ref[r0:r0 + rows, sl].astype(F32) * hx_ref[r0:r0 + rows, sl].astype(F32))
        for i in range(tile // rows):
            r0 = i * rows
            acc = None
            for k in range(C_CONV):
                lo = SUBLANES - (C_CONV - 1) + k + r0
                term = xbuf_ref[lo:lo + rows, sl] * cw_ref[k:k + 1, sl]
                acc = term if acc is None else acc + term
            gate = _silu(zc_ref[r0:r0 + rows, sl].astype(F32)) * bg_ref[r0:r0 + rows, sl].astype(F32)
            o_ref[r0:r0 + rows, sl] = (gate * acc).astype(o_ref.dtype)
        xbuf_ref[0:SUBLANES, sl] = xbuf_ref[tile:tile + SUBLANES, sl]


def _shortconv(h1, conv_w, batch, seq, tile=512):
    t = batch * seq
    nt = seq // tile
    tok = lambda cb: (lambda b, i: (b * nt + i, cb))
    return pl.pallas_call(
        _shortconv_kernel,
        grid=(batch, nt),
        in_specs=[pl.BlockSpec((tile, WIDTH), tok(0)), pl.BlockSpec((tile, WIDTH), tok(1)),
                  pl.BlockSpec((tile, WIDTH), tok(2)), pl.BlockSpec((tile, WIDTH), tok(3)),
                  pl.BlockSpec(conv_w.shape, lambda b, i: (0, 0))],
        out_specs=pl.BlockSpec((tile, WIDTH), tok(0)),
        out_shape=jax.ShapeDtypeStruct((t, WIDTH), BF16),
        scratch_shapes=[pltpu.VMEM((SUBLANES + tile, WIDTH), F32)],
        compiler_params=_cparams("arbitrary", "arbitrary"),
    )(h1, h1, h1, h1, conv_w)


def _attn_kernel(q_ref, kp_ref, kc_ref, vp_ref, vc_ref, zd_ref, o_ref,
                 qs_ref, ks_ref, vs_ref, os_ref, ls_ref, bias_ref):
    first_super = pl.program_id(2) == 0
    nq = N_BACK
    a = lax.broadcasted_iota(jnp.int32, (nq, 2 * nq), 0)
    j = lax.broadcasted_iota(jnp.int32, (nq, 2 * nq), 1)
    in_band = jnp.logical_and(j >= a, j <= a + nq)
    bias_ref[0] = jnp.where(in_band, 0.0, -jnp.inf)
    bias_ref[1] = jnp.where(jnp.logical_and(in_band, j >= nq), 0.0, -jnp.inf)
    scale = D_HEAD_DIM ** -0.5
    blocks = SUPER // nq

    for hh in range(q_ref.shape[1] // D_HEAD_DIM):
        cs = slice(hh * D_HEAD_DIM, (hh + 1) * D_HEAD_DIM)
        qs_ref[...] = q_ref[:, cs].astype(F32) * scale
        ks_ref[0:SUPER] = kp_ref[:, cs].astype(F32)
        ks_ref[SUPER:] = kc_ref[:, cs].astype(F32)
        vs_ref[0:SUPER] = vp_ref[:, cs].astype(F32)
        vs_ref[SUPER:] = vc_ref[:, cs].astype(F32)

        for pi, d in enumerate(DILATIONS):
            per_res = SUPER // d // nq

            def block(i, carry, pi=pi, d=d, per_res=per_res):
                res, sub = i // per_res, i % per_res
                q0 = res + sub * (nq * d)
                k0 = SUPER + q0 - nq * d
                if d == 1:
                    q0 = pl.multiple_of(q0, nq)
                    k0 = pl.multiple_of(k0, nq)
                    qsl, ksl = pl.ds(q0, nq), pl.ds(k0, 2 * nq)
                else:
                    qsl, ksl = pl.ds(q0, nq, stride=d), pl.ds(k0, 2 * nq, stride=d)
                no_prev = jnp.logical_and(first_super, sub == 0)
                qb = qs_ref[qsl, :].astype(BF16)
                kb = ks_ref[ksl, :].astype(BF16)
                vb = vs_ref[ksl, :].astype(BF16)
                s = lax.dot_general(qb, kb, (((1,), (1,)), ((), ())), preferred_element_type=F32)
                s = s + bias_ref[no_prev.astype(jnp.int32)]
                m = jnp.max(s, axis=-1, keepdims=True)
                p = jnp.exp(s - m)
                l = jnp.sum(p, axis=-1, keepdims=True)
                o = jnp.dot(p.astype(BF16), vb, preferred_element_type=F32) * (1.0 / l)
                os_ref[pi, qsl, :] = o
                ls_ref[pi, qsl, :] = jnp.broadcast_to(m + jnp.log(l), (nq, LANES))
                return carry

            lax.fori_loop(0, blocks, block, 0, unroll=8)

        rows = 256

        def combine(c, carry, cs=cs):
            rs = pl.ds(pl.multiple_of(c * rows, rows), rows)
            l1, l4, l16 = ls_ref[0, rs, :], ls_ref[1, rs, :], ls_ref[2, rs, :]
            mx = jnp.maximum(jnp.maximum(l1, l4), l16)
            e1, e4, e16 = jnp.exp(l1 - mx), jnp.exp(l4 - mx), jnp.exp(l16 - mx)
            o = (e1 * os_ref[0, rs, :] + e4 * os_ref[1, rs, :] + e16 * os_ref[2, rs, :]) * (1.0 / (e1 + e4 + e16))
            o_ref[rs, cs] = (_silu(zd_ref[rs, cs].astype(F32)) * o).astype(o_ref.dtype)
            return carry

        lax.fori_loop(0, SUPER // rows, combine, 0)


def _attention(h1, col0, batch, seq, hw=256):
    t = batch * seq
    ns = seq // SUPER
    cb0 = col0 // hw
    per = WIDTH // hw

    def cur(part):
        return lambda b, g, s: (b * ns + s, cb0 + part * per + g)

    def prev(part):
        return lambda b, g, s: (b * ns + jnp.maximum(s - 1, 0), cb0 + part * per + g)

    blk = (SUPER, hw)
    return pl.pallas_call(
        _attn_kernel,
        grid=(batch, per, ns),
        in_specs=[pl.BlockSpec(blk, cur(0)),
                  pl.BlockSpec(blk, prev(1)), pl.BlockSpec(blk, cur(1)),
                  pl.BlockSpec(blk, prev(2)), pl.BlockSpec(blk, cur(2)),
                  pl.BlockSpec(blk, cur(3))],
        out_specs=pl.BlockSpec(blk, lambda b, g, s: (b * ns + s, g)),
        out_shape=jax.ShapeDtypeStruct((t, WIDTH), BF16),
        scratch_shapes=[pltpu.VMEM((SUPER, LANES), F32),
                        pltpu.VMEM((2 * SUPER, LANES), F32),
                        pltpu.VMEM((2 * SUPER, LANES), F32),
                        pltpu.VMEM((len(DILATIONS), SUPER, LANES), F32),
                        pltpu.VMEM((len(DILATIONS), SUPER, LANES), F32),
                        pltpu.VMEM((2, N_BACK, 2 * N_BACK), F32)],
        compiler_params=_cparams("arbitrary", "arbitrary", "arbitrary"),
    )(h1, h1, h1, h1, h1, h1)


def kernel(x, even_norm_g, even_w_in, gmlp_ln_g, gmlp_ln_b, gmlp_ws, gmlp_bs, ssd_conv_w, ssd_conv_b,
           ssd_dt_bias, ssd_a_log, ssd_d, ssd_norm_g, even_w_out, odd_norm_g, odd_w_in, sconv_w,
           odd_w_out, final_norm_g):
    batch, seq, d = x.shape
    assert d == D_MODEL and seq % SUPER == 0
    assert even_norm_g.shape[0] == 1 and odd_norm_g.shape[0] == 1
    t = batch * seq
    x0 = x.reshape(t, d)
    w_in0 = even_w_in.reshape(d, -1)
    w_out0 = even_w_out.reshape(-1, d)
    w_in1 = odd_w_in.reshape(d, -1)
    w_out1 = odd_w_out.reshape(-1, d)

    xn0 = _rmsnorm(x0, even_norm_g[0], BF16)
    n_main = 4 * WIDTH + B_XBC
    h0 = _matmul([xn0], w_in0, BF16, n_cols=n_main)
    w_dt = jnp.pad(w_in0[:, n_main:], ((0, 0), (0, LANES - B_HEADS)))
    dt_raw = _small_matmul(xn0, w_dt)
    x1, xn1 = _mix0(h0, dt_raw, x0, _cast_bf16(w_out0), odd_norm_g[0], batch, seq,
                    gmlp_ln_g[0], gmlp_ln_b[0], gmlp_ws[0], gmlp_bs[0],
                    ssd_conv_w[0], ssd_conv_b[0], ssd_dt_bias[0], ssd_a_log[0], ssd_d[0], ssd_norm_g[0])

    h1 = _matmul([xn1], w_in1, BF16)
    yc = _shortconv(h1, sconv_w[0], batch, seq)
    yd = _attention(h1, 4 * WIDTH, batch, seq)
    x2 = _matmul([yc, yd], w_out1, F32, res=x1, tn=512)

    return _rmsnorm(x2, final_norm_g, F32).reshape(batch, seq, d)
```

```python
import functools

import jax
import jax.numpy as jnp
from jax import lax
from jax.experimental import pallas as pl
from jax.experimental.pallas import tpu as pltpu

F32 = jnp.float32
BF16 = jnp.bfloat16

EPS = 1e-5
D_MODEL = 2048
WIDTH = 2048
A_GROUPS = 8
CHUNK = 128
B_HEAD_DIM = 64
B_HEADS = WIDTH // B_HEAD_DIM
B_GROUPS = 8
B_STATE = 128
B_CONV = 4
B_XBC = WIDTH + 2 * B_GROUPS * B_STATE
HEADS_PER_GROUP = B_HEADS // B_GROUPS
GROUP_W = WIDTH // B_GROUPS
C_CONV = 3
D_HEAD_DIM = 128
D_HEADS = WIDTH // D_HEAD_DIM
N_BACK = 128
DILATIONS = (1, 4, 16)
SUPER = N_BACK * DILATIONS[-1]
LANES = 128
SUBLANES = 8
VMEM_LIMIT = 56 * 1024 * 1024


def _cparams(*sem):
    return pltpu.CompilerParams(dimension_semantics=sem, vmem_limit_bytes=VMEM_LIMIT)


def _silu(z):
    hz = 0.5 * z
    return hz * jnp.tanh(hz) + hz


def _rmsnorm_dt_kernel(x_ref, g_ref, w_ref, o_ref, dt_ref):
    x = x_ref[...]
    ms = jnp.mean(x * x, axis=-1, keepdims=True)
    xn = (x * lax.rsqrt(ms + EPS) * g_ref[...]).astype(o_ref.dtype)
    o_ref[...] = xn
    dt_ref[...] = jnp.dot(xn, w_ref[...].astype(BF16), preferred_element_type=F32)


def _rmsnorm_dt(x2d, g, w_dt, tm=512):
    t, d = x2d.shape
    n = w_dt.shape[1]
    return pl.pallas_call(
        _rmsnorm_dt_kernel,
        grid=(t // tm,),
        in_specs=[pl.BlockSpec((tm, d), lambda i: (i, 0)),
                  pl.BlockSpec((1, d), lambda i: (0, 0)),
                  pl.BlockSpec((d, n), lambda i: (0, 0))],
        out_specs=[pl.BlockSpec((tm, d), lambda i: (i, 0)), pl.BlockSpec((tm, n), lambda i: (i, 0))],
        out_shape=[jax.ShapeDtypeStruct((t, d), BF16), jax.ShapeDtypeStruct((t, n), F32)],
        compiler_params=_cparams("arbitrary"),
    )(x2d, g.reshape(1, d), w_dt)


def _outproj_norm_kernel(a1_ref, a2_ref, w_ref, r_ref, g_ref, o_ref):
    k1 = a1_ref.shape[1]
    x = (jnp.dot(a1_ref[...], w_ref[0:k1, :], preferred_element_type=F32)
         + jnp.dot(a2_ref[...], w_ref[k1:, :], preferred_element_type=F32) + r_ref[...])
    ms = jnp.mean(x * x, axis=-1, keepdims=True)
    o_ref[...] = x * lax.rsqrt(ms + EPS) * g_ref[...]


def _outproj_norm(a1, a2, w_bf16, res, g, tm=512):
    m, k1 = a1.shape
    k2 = a2.shape[1]
    n = w_bf16.shape[1]
    return pl.pallas_call(
        _outproj_norm_kernel,
        grid=(m // tm,),
        in_specs=[pl.BlockSpec((tm, k1), lambda i: (i, 0)),
                  pl.BlockSpec((tm, k2), lambda i: (i, 0)),
                  pl.BlockSpec(w_bf16.shape, lambda i: (0, 0), pipeline_mode=pl.Buffered(1)),
                  pl.BlockSpec((tm, n), lambda i: (i, 0)),
                  pl.BlockSpec((1, n), lambda i: (0, 0))],
        out_specs=pl.BlockSpec((tm, n), lambda i: (i, 0)),
        out_shape=jax.ShapeDtypeStruct((m, n), F32),
        compiler_params=_cparams("arbitrary"),
    )(a1, a2, w_bf16, res, g.reshape(1, n))


def _matmul_kernel(a_ref, w_ref, o_ref, wb_ref, *, cast_rows):
    @pl.when(pl.program_id(1) == 0)
    def _():
        def body(k, carry):
            r = pl.multiple_of(k * cast_rows, cast_rows)
            wb_ref[pl.ds(r, cast_rows), :] = w_ref[pl.ds(r, cast_rows), :].astype(BF16)
            return carry
        lax.fori_loop(0, w_ref.shape[0] // cast_rows, body, 0)

    o_ref[...] = jnp.dot(a_ref[...], wb_ref[...], preferred_element_type=F32).astype(o_ref.dtype)


def _matmul(a, w, n_cols, tm=1024, tn=1024):
    m, k = a.shape
    return pl.pallas_call(
        functools.partial(_matmul_kernel, cast_rows=256),
        grid=(n_cols // tn, m // tm),
        in_specs=[pl.BlockSpec((tm, k), lambda j, i: (i, 0)),
                  pl.BlockSpec((k, tn), lambda j, i: (0, j))],
        out_specs=pl.BlockSpec((tm, tn), lambda j, i: (i, j)),
        out_shape=jax.ShapeDtypeStruct((m, n_cols), BF16),
        scratch_shapes=[pltpu.VMEM((k, tn), BF16)],
        compiler_params=_cparams("arbitrary", "arbitrary"),
    )(a, w)


def _softplus(x):
    return jnp.maximum(x, 0.0) + jnp.log1p(jnp.exp(-jnp.abs(x)))


def _split3(x):
    hi = x.astype(BF16)
    r1 = x - hi.astype(F32)
    mid = r1.astype(BF16)
    lo = (r1 - mid.astype(F32)).astype(BF16)
    return hi, mid, lo


def _mix0_kernel(u_ref, v_ref, za_ref, zb_ref, xbc_ref, dt_ref, xres_ref, wout_ref, g1_ref,
                 lng_ref, lnb_ref, ws_ref, bst_ref, cw_ref, cbias_ref, dtb_ref, alog_ref, dexp_ref, ng_ref,
                 x1_ref, xn1_ref,
                 o_ref, yprev_ref, wsb_ref, xbuf_ref, xs_ref, bm_ref, cm_ref, y_ref, st_ref):
    q = CHUNK
    row = lax.broadcasted_iota(jnp.int32, (q, q), 0)
    col = lax.broadcasted_iota(jnp.int32, (q, q), 1)
    causal = col <= row

    @pl.when(pl.program_id(1) == 0)
    def _():
        o_ref[...] = jnp.zeros_like(o_ref)
        xbuf_ref[0:SUBLANES, :] = jnp.zeros((SUBLANES, B_XBC), F32)
        st_ref[...] = jnp.zeros_like(st_ref)
        for g in range(A_GROUPS):
            wsb_ref[g] = jnp.where(causal, ws_ref[g], 0.0).astype(BF16)

    yprev_ref[...] = o_ref[...]
    ncol = 512
    ss = jnp.zeros((q, 1), F32)
    for nb in range(D_MODEL // ncol):
        cs = slice(nb * ncol, (nb + 1) * ncol)
        xc = jnp.dot(yprev_ref[...], wout_ref[:, cs], preferred_element_type=F32) + xres_ref[:, cs]
        x1_ref[:, cs] = xc
        ss = ss + jnp.sum(xc * xc, axis=-1, keepdims=True)
    rstd1 = lax.rsqrt(ss * (1.0 / D_MODEL) + EPS)
    for nb in range(D_MODEL // ncol):
        cs = slice(nb * ncol, (nb + 1) * ncol)
        xn1_ref[:, cs] = (x1_ref[:, cs] * rstd1 * g1_ref[:, cs]).astype(xn1_ref.dtype)

    v = v_ref[...].astype(F32)
    mu = jnp.mean(v, axis=-1, keepdims=True)
    xc = v - mu
    var = jnp.mean(xc * xc, axis=-1, keepdims=True)
    rstd = lax.rsqrt(var + EPS)
    for g in range(A_GROUPS):
        sl = slice(g * GROUP_W, (g + 1) * GROUP_W)
        vg = v_ref[:, sl].astype(F32)
        vn = ((vg - mu) * rstd * lng_ref[:, sl] + lnb_ref[:, sl]).astype(BF16)
        mixed = jnp.dot(wsb_ref[g], vn, preferred_element_type=F32) + bst_ref[:, g:g + 1]
        z = za_ref[:, sl].astype(F32)
        u = u_ref[:, sl].astype(F32)
        o_ref[:, sl] = (_silu(z) * (u * mixed)).astype(o_ref.dtype)

    cwid = 256
    for j in range(B_XBC // cwid):
        sl = slice(j * cwid, (j + 1) * cwid)
        xbuf_ref[SUBLANES:, sl] = xbc_ref[:, sl].astype(F32)
        acc = cbias_ref[:, sl]
        for k in range(B_CONV):
            lo = SUBLANES - (B_CONV - 1) + k
            acc = acc + xbuf_ref[lo:lo + q, sl] * cw_ref[k:k + 1, sl]
        xbuf_ref[0:SUBLANES, sl] = xbuf_ref[q:q + SUBLANES, sl]
        act = _silu(acc)
        lo = j * cwid
        if lo < WIDTH:
            xs_ref[:, lo:lo + cwid] = act
        elif lo < WIDTH + B_GROUPS * B_STATE:
            bm_ref[:, lo - WIDTH:lo - WIDTH + cwid] = act
        else:
            off = lo - WIDTH - B_GROUPS * B_STATE
            cm_ref[:, off:off + cwid] = act

    dt = _softplus(dt_ref[...] + dtb_ref[...])
    adt = dt * (-jnp.exp(alog_ref[...]))
    tril = jnp.where(causal, 1.0, 0.0).astype(BF16)
    a_cs = sum(jnp.dot(tril, part, preferred_element_type=F32) for part in _split3(adt))
    a_cs_t = a_cs.T
    dt_t = dt.T

    for g in range(B_GROUPS):
        gs = slice(g * B_STATE, (g + 1) * B_STATE)
        bg = bm_ref[:, gs]
        cgb = cm_ref[:, gs].astype(BF16)
        bg_t = bg.T
        cb = lax.dot_general(cgb, bg.astype(BF16), (((1,), (1,)), ((), ())),
                             preferred_element_type=F32)
        for r in range(HEADS_PER_GROUP):
            h = g * HEADS_PER_GROUP + r
            hs = slice(h * B_HEAD_DIM, (h + 1) * B_HEAD_DIM)
            colv = a_cs[:, h:h + 1]
            rowv = a_cs_t[h:h + 1, :]
            dtr = dt_t[h:h + 1, :]
            decay = jnp.exp(jnp.where(causal, colv - rowv, -jnp.inf))
            mh = (cb * decay * dtr).astype(BF16)
            xh = xs_ref[:, hs]
            xhb = xh.astype(BF16)
            state = st_ref[h]
            y = jnp.dot(mh, xhb, preferred_element_type=F32)
            y = y + jnp.exp(colv) * jnp.dot(cgb, state.astype(BF16), preferred_element_type=F32)
            y_ref[:, hs] = y + dexp_ref[:, hs] * xh
            last = a_cs_t[h:h + 1, q - 1:q]
            wrow = jnp.exp(last - rowv) * dtr
            new = jnp.dot((bg_t * wrow).astype(BF16), xhb, preferred_element_type=F32)
            st_ref[h] = state * jnp.exp(last) + new

    for g in range(B_GROUPS):
        sl = slice(g * GROUP_W, (g + 1) * GROUP_W)
        y = y_ref[:, sl] * _silu(zb_ref[:, sl].astype(F32))
        ms = jnp.mean(y * y, axis=-1, keepdims=True)
        o_ref[:, WIDTH + g * GROUP_W:WIDTH + (g + 1) * GROUP_W] = (
            y * lax.rsqrt(ms + EPS) * ng_ref[:, sl]).astype(o_ref.dtype)


def _mix0(h0, dt_raw, x0, w_out, next_norm_g, batch, seq,
          lng, lnb, ws, bs, conv_w, conv_b, dt_bias, a_log, d_skip, norm_g):
    t = batch * seq
    nc = seq // CHUNK
    pad = LANES - B_HEADS
    row = lambda a: a.reshape(1, -1)
    params = [row(lng), row(lnb), ws, bs.T, conv_w, row(conv_b),
              row(jnp.pad(dt_bias, (0, pad))), row(jnp.pad(a_log, (0, pad))),
              row(jnp.repeat(d_skip, B_HEAD_DIM)), row(norm_g)]
    tok = lambda cb: (lambda b, c: (b * nc + jnp.minimum(c, nc - 1), cb))
    lag = lambda b, c: (b * nc + jnp.maximum(c - 1, 0), 0)
    full = lambda a: pl.BlockSpec(a.shape, lambda b, c: (0,) * a.ndim)
    in_specs = [pl.BlockSpec((CHUNK, WIDTH), tok(0)),
                pl.BlockSpec((CHUNK, WIDTH), tok(1)),
                pl.BlockSpec((CHUNK, WIDTH), tok(2)),
                pl.BlockSpec((CHUNK, WIDTH), tok(3)),
                pl.BlockSpec((CHUNK, B_XBC), tok(2)),
                pl.BlockSpec((CHUNK, LANES), tok(0)),
                pl.BlockSpec((CHUNK, D_MODEL), lag),
                pl.BlockSpec(w_out.shape, lambda b, c: (0, 0), pipeline_mode=pl.Buffered(1)),
                full(row(next_norm_g))]
    in_specs += [full(p) for p in params]
    return pl.pallas_call(
        _mix0_kernel,
        grid=(batch, nc + 1),
        in_specs=in_specs,
        out_specs=[pl.BlockSpec((CHUNK, D_MODEL), lag), pl.BlockSpec((CHUNK, D_MODEL), lag)],
        out_shape=[jax.ShapeDtypeStruct((t, D_MODEL), F32), jax.ShapeDtypeStruct((t, D_MODEL), BF16)],
        scratch_shapes=[pltpu.VMEM((CHUNK, 2 * WIDTH), BF16),
                        pltpu.VMEM((CHUNK, 2 * WIDTH), BF16),
                        pltpu.VMEM((A_GROUPS, CHUNK, CHUNK), BF16),
                        pltpu.VMEM((SUBLANES + CHUNK, B_XBC), F32),
                        pltpu.VMEM((CHUNK, WIDTH), F32),
                        pltpu.VMEM((CHUNK, B_GROUPS * B_STATE), F32),
                        pltpu.VMEM((CHUNK, B_GROUPS * B_STATE), F32),
                        pltpu.VMEM((CHUNK, WIDTH), F32),
                        pltpu.VMEM((B_HEADS, B_STATE, B_HEAD_DIM), F32)],
        compiler_params=_cparams("arbitrary", "arbitrary"),
    )(h0, h0, h0, h0, h0, dt_raw, x0, w_out, row(next_norm_g), *params)


def _cast_kernel(w_ref, o_ref):
    o_ref[...] = w_ref[...].astype(o_ref.dtype)


def _cast_bf16(w, tr=512):
    r, c = w.shape
    return pl.pallas_call(
        _cast_kernel,
        grid=(r // tr,),
        in_specs=[pl.BlockSpec((tr, c), lambda i: (i, 0))],
        out_specs=pl.BlockSpec((tr, c), lambda i: (i, 0)),
        out_shape=jax.ShapeDtypeStruct((r, c), BF16),
        compiler_params=_cparams("arbitrary"),
    )(w)


def _shortconv_kernel(bg_ref, cg_ref, hx_ref, zc_ref, cw_ref, o_ref, xbuf_ref):
    tile = o_ref.shape[0]

    @pl.when(pl.program_id(1) == 0)
    def _():
        xbuf_ref[0:SUBLANES, :] = jnp.zeros((SUBLANES, WIDTH), F32)

    cwid, rows = 256, 128
    for j in range(WIDTH // cwid):
        sl = slice(j * cwid, (j + 1) * cwid)
        for i in range(tile // rows):
            r0 = i * rows
            xbuf_ref[SUBLANES + r0:SUBLANES + r0 + rows, sl] = (
                cg_ref[r0:r0 + rows, sl].astype(F32) * hx_ref[r0:r0 + rows, sl].astype(F32))
        for i in range(tile // rows):
            r0 = i * rows
            acc = None
            for k in range(C_CONV):
                lo = SUBLANES - (C_CONV - 1) + k + r0
                term = xbuf_ref[lo:lo + rows, sl] * cw_ref[k:k + 1, sl]
                acc = term if acc is None else acc + term
            gate = _silu(zc_ref[r0:r0 + rows, sl].astype(F32)) * bg_ref[r0:r0 + rows, sl].astype(F32)
            o_ref[r0:r0 + rows, sl] = (gate * acc).astype(o_ref.dtype)
        xbuf_ref[0:SUBLANES, sl] = xbuf_ref[tile:tile + SUBLANES, sl]


def _shortconv(h1, conv_w, batch, seq, tile=512):
    t = batch * seq
    nt = seq // tile
    tok = lambda cb: (lambda b, i: (b * nt + i, cb))
    return pl.pallas_call(
        _shortconv_kernel,
        grid=(batch, nt),
        in_specs=[pl.BlockSpec((tile, WIDTH), tok(0)), pl.BlockSpec((tile, WIDTH), tok(1)),
                  pl.BlockSpec((tile, WIDTH), tok(2)), pl.BlockSpec((tile, WIDTH), tok(3)),
                  pl.BlockSpec(conv_w.shape, lambda b, i: (0, 0))],
        out_specs=pl.BlockSpec((tile, WIDTH), tok(0)),
        out_shape=jax.ShapeDtypeStruct((t, WIDTH), BF16),
        scratch_shapes=[pltpu.VMEM((SUBLANES + tile, WIDTH), F32)],
        compiler_params=_cparams("arbitrary", "arbitrary"),
    )(h1, h1, h1, h1, conv_w)


def _attn_kernel(q_ref, kp_ref, kc_ref, vp_ref, vc_ref, zd_ref, o_ref,
                 qs_ref, ks_ref, vs_ref, os_ref, ls_ref, ms_ref, bias_ref, s_scr, p_scr, m_scr):
    first_super = pl.program_id(2) == 0
    nq = N_BACK
    heads = q_ref.shape[1] // D_HEAD_DIM
    blocks = SUPER // nq
    a = lax.broadcasted_iota(jnp.int32, (nq, 2 * nq), 0)
    j = lax.broadcasted_iota(jnp.int32, (nq, 2 * nq), 1)
    in_band = jnp.logical_and(j >= a, j <= a + nq)
    bias_ref[0] = jnp.where(in_band, 0.0, -jnp.inf)
    bias_ref[1] = jnp.where(jnp.logical_and(in_band, j >= nq), 0.0, -jnp.inf)
    s_scr[...] = jnp.zeros_like(s_scr)
    p_scr[...] = jnp.zeros_like(p_scr)
    m_scr[...] = jnp.zeros_like(m_scr)
    qscale = D_HEAD_DIM ** -0.5 * 1.4426950408889634
    ones_cols = jnp.ones((2 * nq, LANES), BF16)

    for hh in range(heads):
        cs = slice(hh * D_HEAD_DIM, (hh + 1) * D_HEAD_DIM)
        qs_ref[hh] = q_ref[:, cs].astype(F32) * qscale
        ks_ref[hh, 0:SUPER] = kp_ref[:, cs].astype(F32)
        ks_ref[hh, SUPER:] = kc_ref[:, cs].astype(F32)
        vs_ref[hh, 0:SUPER] = vp_ref[:, cs].astype(F32)
        vs_ref[hh, SUPER:] = vc_ref[:, cs].astype(F32)

    def slices(i, d):
        per_res = SUPER // d // nq
        res, sub = i // per_res, i % per_res
        q0 = res + sub * (nq * d)
        k0 = SUPER + q0 - nq * d
        if d == 1:
            return pl.ds(pl.multiple_of(q0, nq), nq), pl.ds(pl.multiple_of(k0, nq), 2 * nq), sub
        return pl.ds(q0, nq, stride=d), pl.ds(k0, 2 * nq, stride=d), sub

    def step(t, carry):
        i_pv = jnp.clip(t - 2, 0, blocks - 1)
        i_sm = jnp.clip(t - 1, 0, blocks - 1)
        i_qk = jnp.minimum(t, blocks - 1)
        del i_sm
        combos = [(hh, pi, d) for hh in range(heads) for pi, d in enumerate(DILATIONS)]
        loaded = []
        for hh, pi, d in combos:
            slot = hh * len(DILATIONS) + pi
            _, ksl_pv, _ = slices(i_pv, d)
            qsl, ksl, sub = slices(i_qk, d)
            no_prev = jnp.logical_and(first_super, sub == 0)
            loaded.append((p_scr[slot], m_scr[slot], vs_ref[hh, ksl_pv, :], s_scr[slot],
                           qs_ref[hh, qsl, :], ks_ref[hh, ksl, :], bias_ref[no_prev.astype(jnp.int32)]))
        results = []
        for p_old, m_old, v_f32, s_old, q_f32, k_f32, bias in loaded:
            vb = jnp.concatenate([v_f32.astype(BF16), ones_cols], axis=1)
            o2 = jnp.dot(p_old, vb, preferred_element_type=F32)
            m = jnp.max(s_old, axis=-1, keepdims=True)
            p_new = jnp.exp2(s_old - m).astype(BF16)
            s_new = lax.dot_general(q_f32.astype(BF16), k_f32.astype(BF16), (((1,), (1,)), ((), ())),
                                    preferred_element_type=F32) + bias
            results.append((o2, m_old, p_new, jnp.broadcast_to(m, (nq, LANES)), s_new))
        for (hh, pi, d), (o2, m_old, p_new, m_new, s_new) in zip(combos, results):
            slot = hh * len(DILATIONS) + pi
            qsl_pv, _, _ = slices(i_pv, d)
            p_scr[slot] = p_new
            m_scr[slot] = m_new
            s_scr[slot] = s_new
            os_ref[slot, qsl_pv, :] = o2[:, :LANES]
            ls_ref[slot, qsl_pv, :] = o2[:, LANES:]
            ms_ref[slot, qsl_pv, :] = m_old
        return carry

    lax.fori_loop(0, blocks + 2, step, 0)

    rows = 256
    npat = len(DILATIONS)
    for hh in range(heads):
        cs = slice(hh * D_HEAD_DIM, (hh + 1) * D_HEAD_DIM)

        def combine(c, carry, hh=hh, cs=cs):
            rs = pl.ds(pl.multiple_of(c * rows, rows), rows)
            ms = [ms_ref[hh * npat + pi, rs, :] for pi in range(npat)]
            mx = jnp.maximum(jnp.maximum(ms[0], ms[1]), ms[2])
            num = jnp.zeros((rows, LANES), F32)
            den = jnp.zeros((rows, LANES), F32)
            for pi in range(npat):
                e = jnp.exp2(ms[pi] - mx)
                num = num + e * os_ref[hh * npat + pi, rs, :]
                den = den + e * ls_ref[hh * npat + pi, rs, :]
            o_ref[rs, cs] = (_silu(zd_ref[rs, cs].astype(F32)) * (num / den)).astype(o_ref.dtype)
            return carry

        lax.fori_loop(0, SUPER // rows, combine, 0)


def _attention(h1, col0, batch, seq, hw=256):
    t = batch * seq
    ns = seq // SUPER
    cb0 = col0 // hw
    per = WIDTH // hw
    heads = hw // D_HEAD_DIM
    slots = heads * len(DILATIONS)

    def cur(part):
        return lambda b, g, s: (b * ns + s, cb0 + part * per + g)

    def prev(part):
        return lambda b, g, s: (b * ns + jnp.maximum(s - 1, 0), cb0 + part * per + g)

    blk = (SUPER, hw)
    return pl.pallas_call(
        _attn_kernel,
        grid=(batch, per, ns),
        in_specs=[pl.BlockSpec(blk, cur(0)),
                  pl.BlockSpec(blk, prev(1)), pl.BlockSpec(blk, cur(1)),
                  pl.BlockSpec(blk, prev(2)), pl.BlockSpec(blk, cur(2)),
                  pl.BlockSpec(blk, cur(3))],
        out_specs=pl.BlockSpec(blk, lambda b, g, s: (b * ns + s, g)),
        out_shape=jax.ShapeDtypeStruct((t, WIDTH), BF16),
        scratch_shapes=[pltpu.VMEM((heads, SUPER, LANES), F32),
                        pltpu.VMEM((heads, 2 * SUPER, LANES), F32),
                        pltpu.VMEM((heads, 2 * SUPER, LANES), F32),
                        pltpu.VMEM((slots, SUPER, LANES), F32),
                        pltpu.VMEM((slots, SUPER, LANES), F32),
                        pltpu.VMEM((slots, SUPER, LANES), F32),
                        pltpu.VMEM((2, N_BACK, 2 * N_BACK), F32),
                        pltpu.VMEM((slots, N_BACK, 2 * N_BACK), F32),
                        pltpu.VMEM((slots, N_BACK, 2 * N_BACK), BF16),
                        pltpu.VMEM((slots, N_BACK, LANES), F32)],
        compiler_params=_cparams("arbitrary", "arbitrary", "arbitrary"),
    )(h1, h1, h1, h1, h1, h1)


R4 = DILATIONS[1]
ROWS4 = SUPER // R4


def _attn4_kernel(q_ref, kp_ref, kc_ref, vp_ref, vc_ref, zd_ref, o_ref,
                  qn_ref, kn_ref, vn_ref, q4_ref, k4_ref, v4_ref, os_ref, ls_ref, ms_ref, fin_ref, bias_ref):
    first_super = pl.program_id(2) == 0
    nq = N_BACK
    blocks = SUPER // nq
    a = lax.broadcasted_iota(jnp.int32, (nq, 2 * nq), 0)
    j = lax.broadcasted_iota(jnp.int32, (nq, 2 * nq), 1)
    in_band = jnp.logical_and(j >= a, j <= a + nq)
    bias_ref[0] = jnp.where(in_band, 0.0, -jnp.inf)
    bias_ref[1] = jnp.where(jnp.logical_and(in_band, j >= nq), 0.0, -jnp.inf)
    qscale = D_HEAD_DIM ** -0.5 * 1.4426950408889634
    ones_cols = jnp.ones((2 * nq, LANES), BF16)

    qn_ref[...] = q_ref[...].astype(F32) * qscale
    kn_ref[0:SUPER] = kp_ref[...].astype(F32)
    kn_ref[SUPER:] = kc_ref[...].astype(F32)
    vn_ref[0:SUPER] = vp_ref[...].astype(F32)
    vn_ref[SUPER:] = vc_ref[...].astype(F32)
    for r in range(R4):
        q4_ref[r * ROWS4:(r + 1) * ROWS4] = qn_ref[pl.ds(r, ROWS4, stride=R4), :]
        for half in range(2):
            dst = slice(half * SUPER + r * ROWS4, half * SUPER + (r + 1) * ROWS4)
            k4_ref[dst] = kn_ref[pl.ds(half * SUPER + r, ROWS4, stride=R4), :]
            v4_ref[dst] = vn_ref[pl.ds(half * SUPER + r, ROWS4, stride=R4), :]

    def softmax_block(qf, kf, vf, no_prev):
        s = lax.dot_general(qf.astype(BF16), kf.astype(BF16), (((1,), (1,)), ((), ())),
                            preferred_element_type=F32) + bias_ref[no_prev.astype(jnp.int32)]
        m = jnp.max(s, axis=-1, keepdims=True)
        p = jnp.exp2(s - m).astype(BF16)
        o2 = jnp.dot(p, jnp.concatenate([vf.astype(BF16), ones_cols], axis=1), preferred_element_type=F32)
        return o2[:, :LANES], o2[:, LANES:], jnp.broadcast_to(m, (nq, LANES))

    def block(i, carry):
        q1 = pl.ds(pl.multiple_of(i * nq, nq), nq)
        k1 = pl.ds(pl.multiple_of(SUPER + (i - 1) * nq, nq), 2 * nq)
        in1 = (qn_ref[q1, :], kn_ref[k1, :], vn_ref[k1, :], jnp.logical_and(first_super, i == 0))
        res, sub = i // (ROWS4 // nq), i % (ROWS4 // nq)
        q4 = pl.ds(pl.multiple_of(res * ROWS4 + sub * nq, nq), nq)
        kc4 = pl.ds(pl.multiple_of(SUPER + res * ROWS4 + sub * nq, nq), nq)
        kp4 = pl.ds(pl.multiple_of(jnp.where(sub > 0, SUPER + res * ROWS4 + (sub - 1) * nq,
                                             res * ROWS4 + ROWS4 - nq), nq), nq)
        in4 = (q4_ref[q4, :],
               jnp.concatenate([k4_ref[kp4, :], k4_ref[kc4, :]], axis=0),
               jnp.concatenate([v4_ref[kp4, :], v4_ref[kc4, :]], axis=0),
               jnp.logical_and(first_super, sub == 0))
        lo, hi = i % R4, i // R4
        q16 = pl.ds(lo * ROWS4 + hi, nq, stride=R4)
        kp16 = pl.ds(lo * ROWS4 + hi, nq, stride=R4)
        kc16 = pl.ds(SUPER + lo * ROWS4 + hi, nq, stride=R4)
        in16 = (q4_ref[q16, :],
                jnp.concatenate([k4_ref[kp16, :], k4_ref[kc16, :]], axis=0),
                jnp.concatenate([v4_ref[kp16, :], v4_ref[kc16, :]], axis=0),
                first_super)
        outs = [softmax_block(*args) for args in (in1, in4, in16)]
        for pi, qsl in enumerate((q1, q4, q16)):
            os_ref[pi, qsl, :] = outs[pi][0]
            ls_ref[pi, qsl, :] = outs[pi][1]
            ms_ref[pi, qsl, :] = outs[pi][2]
        return carry

    lax.fori_loop(0, blocks, block, 0, unroll=4)

    rows = 256

    def combine(c, carry):
        home = pl.ds(pl.multiple_of(c * rows, rows), rows)
        r, part = c // (ROWS4 // rows), c % (ROWS4 // rows)
        tok = pl.ds(r + part * (rows * R4), rows, stride=R4)
        ms = [ms_ref[0, tok, :], ms_ref[1, home, :], ms_ref[2, home, :]]
        os = [os_ref[0, tok, :], os_ref[1, home, :], os_ref[2, home, :]]
        ls = [ls_ref[0, tok, :], ls_ref[1, home, :], ls_ref[2, home, :]]
        mx = jnp.maximum(jnp.maximum(ms[0], ms[1]), ms[2])
        num = jnp.zeros((rows, LANES), F32)
        den = jnp.zeros((rows, LANES), F32)
        for pi in range(len(DILATIONS)):
            e = jnp.exp2(ms[pi] - mx)
            num = num + e * os[pi]
            den = den + e * ls[pi]
        fin_ref[tok, :] = num / den
        return carry

    lax.fori_loop(0, SUPER // rows, combine, 0)

    def gate(c, carry):
        rs = pl.ds(pl.multiple_of(c * rows, rows), rows)
        o_ref[rs, :] = (_silu(zd_ref[rs, :].astype(F32)) * fin_ref[rs, :]).astype(o_ref.dtype)
        return carry

    lax.fori_loop(0, SUPER // rows, gate, 0)


def _attention4(h1, col0, batch, seq):
    t = batch * seq
    ns = seq // SUPER
    hw = D_HEAD_DIM
    cb0 = col0 // hw
    per = WIDTH // hw
    npat = len(DILATIONS)

    def cur(part):
        return lambda b, g, s: (b * ns + s, cb0 + part * per + g)

    def prev(part):
        return lambda b, g, s: (b * ns + jnp.maximum(s - 1, 0), cb0 + part * per + g)

    blk = (SUPER, hw)
    return pl.pallas_call(
        _attn4_kernel,
        grid=(batch, per, ns),
        in_specs=[pl.BlockSpec(blk, cur(0)),
                  pl.BlockSpec(blk, prev(1)), pl.BlockSpec(blk, cur(1)),
                  pl.BlockSpec(blk, prev(2)), pl.BlockSpec(blk, cur(2)),
                  pl.BlockSpec(blk, cur(3))],
        out_specs=pl.BlockSpec(blk, lambda b, g, s: (b * ns + s, g)),
        out_shape=jax.ShapeDtypeStruct((t, WIDTH), BF16),
        scratch_shapes=[pltpu.VMEM((SUPER, LANES), F32),
                        pltpu.VMEM((2 * SUPER, LANES), F32),
                        pltpu.VMEM((2 * SUPER, LANES), F32),
                        pltpu.VMEM((SUPER, LANES), F32),
                        pltpu.VMEM((2 * SUPER, LANES), F32),
                        pltpu.VMEM((2 * SUPER, LANES), F32),
                        pltpu.VMEM((npat, SUPER, LANES), F32),
                        pltpu.VMEM((npat, SUPER, LANES), F32),
                        pltpu.VMEM((npat, SUPER, LANES), F32),
                        pltpu.VMEM((SUPER, LANES), F32),
                        pltpu.VMEM((2, N_BACK, 2 * N_BACK), F32)],
        compiler_params=_cparams("arbitrary", "arbitrary", "arbitrary"),
    )(h1, h1, h1, h1, h1, h1)


def kernel(x, even_norm_g, even_w_in, gmlp_ln_g, gmlp_ln_b, gmlp_ws, gmlp_bs, ssd_conv_w, ssd_conv_b,
           ssd_dt_bias, ssd_a_log, ssd_d, ssd_norm_g, even_w_out, odd_norm_g, odd_w_in, sconv_w,
           odd_w_out, final_norm_g):
    batch, seq, d = x.shape
    assert d == D_MODEL and seq % SUPER == 0
    assert even_norm_g.shape[0] == 1 and odd_norm_g.shape[0] == 1
    t = batch * seq
    x0 = x.reshape(t, d)
    w_in0 = even_w_in.reshape(d, -1)
    w_out0 = even_w_out.reshape(-1, d)
    w_in1 = odd_w_in.reshape(d, -1)
    w_out1 = odd_w_out.reshape(-1, d)

    n_main = 4 * WIDTH + B_XBC
    w_dt = jnp.pad(w_in0[:, n_main:], ((0, 0), (0, LANES - B_HEADS)))
    xn0, dt_raw = _rmsnorm_dt(x0, even_norm_g[0], w_dt)
    h0 = _matmul(xn0, w_in0, n_main)
    x1, xn1 = _mix0(h0, dt_raw, x0, _cast_bf16(w_out0), odd_norm_g[0], batch, seq,
                    gmlp_ln_g[0], gmlp_ln_b[0], gmlp_ws[0], gmlp_bs[0],
                    ssd_conv_w[0], ssd_conv_b[0], ssd_dt_bias[0], ssd_a_log[0], ssd_d[0], ssd_norm_g[0])

    h1 = _matmul(xn1, w_in1, w_in1.shape[1])
    yc = _shortconv(h1, sconv_w[0], batch, seq)
    yd = _attention4(h1, 4 * WIDTH, batch, seq)
    return _outproj_norm(yc, yd, _cast_bf16(w_out1), x1, final_norm_g).reshape(batch, seq, d)
```

```python
import functools

import jax
import jax.numpy as jnp
from jax import lax
from jax.experimental import pallas as pl
from jax.experimental.pallas import tpu as pltpu

F32 = jnp.float32
BF16 = jnp.bfloat16

EPS = 1e-5
D_MODEL = 2048
WIDTH = 2048
A_GROUPS = 8
CHUNK = 128
MIX0_CHUNKS = 1
B_HEAD_DIM = 64
B_HEADS = WIDTH // B_HEAD_DIM
B_GROUPS = 8
B_STATE = 128
B_CONV = 4
B_XBC = WIDTH + 2 * B_GROUPS * B_STATE
HEADS_PER_GROUP = B_HEADS // B_GROUPS
GROUP_W = WIDTH // B_GROUPS
C_CONV = 3
D_HEAD_DIM = 128
D_HEADS = WIDTH // D_HEAD_DIM
N_BACK = 128
DILATIONS = (1, 4, 16)
SUPER = N_BACK * DILATIONS[-1]
LANES = 128
SUBLANES = 8
VMEM_LIMIT = 56 * 1024 * 1024


def _cparams(*sem):
    return pltpu.CompilerParams(dimension_semantics=sem, vmem_limit_bytes=VMEM_LIMIT)


def _silu(z):
    hz = 0.5 * z
    return hz * jnp.tanh(hz) + hz


def _rmsnorm_dt_kernel(x_ref, g_ref, w_ref, o_ref, dt_ref):
    x = x_ref[...]
    ms = jnp.mean(x * x, axis=-1, keepdims=True)
    xn = (x * lax.rsqrt(ms + EPS) * g_ref[...]).astype(o_ref.dtype)
    o_ref[...] = xn
    dt_ref[...] = jnp.dot(xn, w_ref[...].astype(BF16), preferred_element_type=F32)


def _rmsnorm_dt(x2d, g, w_dt, tm=512):
    t, d = x2d.shape
    n = w_dt.shape[1]
    return pl.pallas_call(
        _rmsnorm_dt_kernel,
        grid=(t // tm,),
        in_specs=[pl.BlockSpec((tm, d), lambda i: (i, 0)),
                  pl.BlockSpec((1, d), lambda i: (0, 0)),
                  pl.BlockSpec((d, n), lambda i: (0, 0))],
        out_specs=[pl.BlockSpec((tm, d), lambda i: (i, 0)), pl.BlockSpec((tm, n), lambda i: (i, 0))],
        out_shape=[jax.ShapeDtypeStruct((t, d), BF16), jax.ShapeDtypeStruct((t, n), F32)],
        compiler_params=_cparams("arbitrary"),
    )(x2d, g.reshape(1, d), w_dt)


def _outproj_norm_kernel(a1_ref, a2_ref, w_ref, r_ref, g_ref, o_ref):
    k1 = a1_ref.shape[1]
    x = (jnp.dot(a1_ref[...], w_ref[0:k1, :], preferred_element_type=F32)
         + jnp.dot(a2_ref[...], w_ref[k1:, :], preferred_element_type=F32) + r_ref[...])
    ms = jnp.mean(x * x, axis=-1, keepdims=True)
    o_ref[...] = x * lax.rsqrt(ms + EPS) * g_ref[...]


def _outproj_norm(a1, a2, w_bf16, res, g, tm=512):
    m, k1 = a1.shape
    k2 = a2.shape[1]
    n = w_bf16.shape[1]
    return pl.pallas_call(
        _outproj_norm_kernel,
        grid=(m // tm,),
        in_specs=[pl.BlockSpec((tm, k1), lambda i: (i, 0)),
                  pl.BlockSpec((tm, k2), lambda i: (i, 0)),
                  pl.BlockSpec(w_bf16.shape, lambda i: (0, 0), pipeline_mode=pl.Buffered(1)),
                  pl.BlockSpec((tm, n), lambda i: (i, 0)),
                  pl.BlockSpec((1, n), lambda i: (0, 0))],
        out_specs=pl.BlockSpec((tm, n), lambda i: (i, 0)),
        out_shape=jax.ShapeDtypeStruct((m, n), F32),
        compiler_params=_cparams("arbitrary"),
    )(a1, a2, w_bf16, res, g.reshape(1, n))


def _matmul_kernel(a_ref, w_ref, o_ref, wb_ref, *, cast_rows, w_is_nk):
    @pl.when(pl.program_id(1) == 0)
    def _():
        def body(k, carry):
            r = pl.multiple_of(k * cast_rows, cast_rows)
            wb_ref[pl.ds(r, cast_rows), :] = w_ref[pl.ds(r, cast_rows), :].astype(BF16)
            return carry
        lax.fori_loop(0, w_ref.shape[0] // cast_rows, body, 0)

    dims = (((1,), (1,)), ((), ())) if w_is_nk else (((1,), (0,)), ((), ()))
    o_ref[...] = lax.dot_general(a_ref[...], wb_ref[...], dims, preferred_element_type=F32).astype(o_ref.dtype)


def _matmul(a, w, n_cols, w_is_nk=False, tm=2048, tn=1024):
    m, k = a.shape
    if w_is_nk:
        w_block = pl.BlockSpec((tn, k), lambda j, i: (j, 0), pipeline_mode=pl.Buffered(1))
    else:
        w_block = pl.BlockSpec((k, tn), lambda j, i: (0, j), pipeline_mode=pl.Buffered(1))
    return pl.pallas_call(
        functools.partial(_matmul_kernel, cast_rows=256, w_is_nk=w_is_nk),
        grid=(n_cols // tn, m // tm),
        in_specs=[pl.BlockSpec((tm, k), lambda j, i: (i, 0)), w_block],
        out_specs=pl.BlockSpec((tm, tn), lambda j, i: (i, j)),
        out_shape=jax.ShapeDtypeStruct((m, n_cols), BF16),
        scratch_shapes=[pltpu.VMEM(w_block.block_shape, BF16)],
        compiler_params=_cparams("arbitrary", "arbitrary"),
    )(a, w)


def _softplus(x):
    return jnp.maximum(x, 0.0) + jnp.log1p(jnp.exp(-jnp.abs(x)))


def _split3(x):
    hi = x.astype(BF16)
    r1 = x - hi.astype(F32)
    mid = r1.astype(BF16)
    lo = (r1 - mid.astype(F32)).astype(BF16)
    return hi, mid, lo


def _mix0_kernel(u_ref, v_ref, za_ref, zb_ref, xbc_ref, dt_ref, xres_ref, wout_ref, g1_ref,
                 lng_ref, lnb_ref, ws_ref, bst_ref, cw_ref, cbias_ref, dtb_ref, alog_ref, dexp_ref, ng_ref,
                 x1_ref, xn1_ref,
                 o_ref, yprev_ref, wsb_ref, xbuf_ref, xs_ref, bm_ref, cm_ref, y_ref, st_ref):
    q = CHUNK
    row = lax.broadcasted_iota(jnp.int32, (q, q), 0)
    col = lax.broadcasted_iota(jnp.int32, (q, q), 1)
    causal = col <= row

    @pl.when(pl.program_id(1) == 0)
    def _():
        o_ref[...] = jnp.zeros_like(o_ref)
        xbuf_ref[0:SUBLANES, :] = jnp.zeros((SUBLANES, B_XBC), F32)
        st_ref[...] = jnp.zeros_like(st_ref)
        for g in range(A_GROUPS):
            wsb_ref[g] = jnp.where(causal, ws_ref[g], 0.0).astype(BF16)

    yprev_ref[...] = o_ref[...]
    ncol = 512
    ss = jnp.zeros((o_ref.shape[0], 1), F32)
    for nb in range(D_MODEL // ncol):
        cs = slice(nb * ncol, (nb + 1) * ncol)
        xc = jnp.dot(yprev_ref[...], wout_ref[:, cs], preferred_element_type=F32) + xres_ref[:, cs]
        x1_ref[:, cs] = xc
        ss = ss + jnp.sum(xc * xc, axis=-1, keepdims=True)
    rstd1 = lax.rsqrt(ss * (1.0 / D_MODEL) + EPS)
    for nb in range(D_MODEL // ncol):
        cs = slice(nb * ncol, (nb + 1) * ncol)
        xn1_ref[:, cs] = (x1_ref[:, cs] * rstd1 * g1_ref[:, cs]).astype(xn1_ref.dtype)

    tril = jnp.where(causal, 1.0, 0.0).astype(BF16)
    for ci in range(o_ref.shape[0] // q):
        rs = slice(ci * q, (ci + 1) * q)
        _mix0_chunk(rs, causal, tril, u_ref, v_ref, za_ref, zb_ref, xbc_ref, dt_ref,
                    lng_ref, lnb_ref, bst_ref, cw_ref, cbias_ref, dtb_ref, alog_ref, dexp_ref, ng_ref,
                    o_ref, wsb_ref, xbuf_ref, xs_ref, bm_ref, cm_ref, y_ref, st_ref)
    rows = o_ref.shape[0]
    xbuf_ref[0:SUBLANES, :] = xbuf_ref[rows:rows + SUBLANES, :]


def _mix0_chunk(rs, causal, tril, u_ref, v_ref, za_ref, zb_ref, xbc_ref, dt_ref,
                lng_ref, lnb_ref, bst_ref, cw_ref, cbias_ref, dtb_ref, alog_ref, dexp_ref, ng_ref,
                o_ref, wsb_ref, xbuf_ref, xs_ref, bm_ref, cm_ref, y_ref, st_ref):
    q = CHUNK
    v = v_ref[rs, :].astype(F32)
    mu = jnp.mean(v, axis=-1, keepdims=True)
    xc = v - mu
    var = jnp.mean(xc * xc, axis=-1, keepdims=True)
    rstd = lax.rsqrt(var + EPS)
    for g in range(A_GROUPS):
        sl = slice(g * GROUP_W, (g + 1) * GROUP_W)
        vg = v_ref[rs, sl].astype(F32)
        vn = ((vg - mu) * rstd * lng_ref[:, sl] + lnb_ref[:, sl]).astype(BF16)
        mixed = jnp.dot(wsb_ref[g], vn, preferred_element_type=F32) + bst_ref[:, g:g + 1]
        z = za_ref[rs, sl].astype(F32)
        u = u_ref[rs, sl].astype(F32)
        o_ref[rs, sl] = (_silu(z) * (u * mixed)).astype(o_ref.dtype)

    cwid = 256
    base = SUBLANES + rs.start
    for j in range(B_XBC // cwid):
        sl = slice(j * cwid, (j + 1) * cwid)
        xbuf_ref[base:base + q, sl] = xbc_ref[rs, sl].astype(F32)
        acc = cbias_ref[:, sl]
        for k in range(B_CONV):
            lo = base - (B_CONV - 1) + k
            acc = acc + xbuf_ref[lo:lo + q, sl] * cw_ref[k:k + 1, sl]
        act = _silu(acc)
        lo = j * cwid
        if lo < WIDTH:
            xs_ref[:, lo:lo + cwid] = act
        elif lo < WIDTH + B_GROUPS * B_STATE:
            bm_ref[:, lo - WIDTH:lo - WIDTH + cwid] = act
        else:
            off = lo - WIDTH - B_GROUPS * B_STATE
            cm_ref[:, off:off + cwid] = act

    dt = _softplus(dt_ref[rs, :] + dtb_ref[...])
    adt = dt * (-jnp.exp(alog_ref[...]))
    a_cs = sum(jnp.dot(tril, part, preferred_element_type=F32) for part in _split3(adt))
    a_cs_t = a_cs.T
    dt_t = dt.T

    for g in range(B_GROUPS):
        gs = slice(g * B_STATE, (g + 1) * B_STATE)
        bg = bm_ref[:, gs]
        cgb = cm_ref[:, gs].astype(BF16)
        bg_t = bg.T
        cb = lax.dot_general(cgb, bg.astype(BF16), (((1,), (1,)), ((), ())),
                             preferred_element_type=F32)
        for r in range(HEADS_PER_GROUP):
            h = g * HEADS_PER_GROUP + r
            hs = slice(h * B_HEAD_DIM, (h + 1) * B_HEAD_DIM)
            colv = a_cs[:, h:h + 1]
            rowv = a_cs_t[h:h + 1, :]
            dtr = dt_t[h:h + 1, :]
            decay = jnp.exp(jnp.where(causal, colv - rowv, -jnp.inf))
            mh = (cb * decay * dtr).astype(BF16)
            xh = xs_ref[:, hs]
            xhb = xh.astype(BF16)
            state = st_ref[h]
            y = jnp.dot(mh, xhb, preferred_element_type=F32)
            y = y + jnp.exp(colv) * jnp.dot(cgb, state.astype(BF16), preferred_element_type=F32)
            y_ref[:, hs] = y + dexp_ref[:, hs] * xh
            last = a_cs_t[h:h + 1, q - 1:q]
            wrow = jnp.exp(last - rowv) * dtr
            new = jnp.dot((bg_t * wrow).astype(BF16), xhb, preferred_element_type=F32)
            st_ref[h] = state * jnp.exp(last) + new

    for g in range(B_GROUPS):
        sl = slice(g * GROUP_W, (g + 1) * GROUP_W)
        y = y_ref[:, sl] * _silu(zb_ref[rs, sl].astype(F32))
        ms = jnp.mean(y * y, axis=-1, keepdims=True)
        o_ref[rs, WIDTH + g * GROUP_W:WIDTH + (g + 1) * GROUP_W] = (
            y * lax.rsqrt(ms + EPS) * ng_ref[:, sl]).astype(o_ref.dtype)


def _mix0(h0, dt_raw, x0, w_out, next_norm_g, batch, seq,
          lng, lnb, ws, bs, conv_w, conv_b, dt_bias, a_log, d_skip, norm_g):
    t = batch * seq
    rows = MIX0_CHUNKS * CHUNK
    nc = seq // rows
    pad = LANES - B_HEADS
    row = lambda a: a.reshape(1, -1)
    params = [row(lng), row(lnb), ws, bs.T, conv_w, row(conv_b),
              row(jnp.pad(dt_bias, (0, pad))), row(jnp.pad(a_log, (0, pad))),
              row(jnp.repeat(d_skip, B_HEAD_DIM)), row(norm_g)]
    tok = lambda cb: (lambda b, c: (b * nc + jnp.minimum(c, nc - 1), cb))
    lag = lambda b, c: (b * nc + jnp.maximum(c - 1, 0), 0)
    full = lambda a: pl.BlockSpec(a.shape, lambda b, c: (0,) * a.ndim)
    in_specs = [pl.BlockSpec((rows, WIDTH), tok(0)),
                pl.BlockSpec((rows, WIDTH), tok(1)),
                pl.BlockSpec((rows, WIDTH), tok(2)),
                pl.BlockSpec((rows, WIDTH), tok(3)),
                pl.BlockSpec((rows, B_XBC), tok(2)),
                pl.BlockSpec((rows, LANES), tok(0)),
                pl.BlockSpec((rows, D_MODEL), lag),
                pl.BlockSpec(w_out.shape, lambda b, c: (0, 0), pipeline_mode=pl.Buffered(1)),
                full(row(next_norm_g))]
    in_specs += [full(p) for p in params]
    return pl.pallas_call(
        _mix0_kernel,
        grid=(batch, nc + 1),
        in_specs=in_specs,
        out_specs=[pl.BlockSpec((rows, D_MODEL), lag), pl.BlockSpec((rows, D_MODEL), lag)],
        out_shape=[jax.ShapeDtypeStruct((t, D_MODEL), F32), jax.ShapeDtypeStruct((t, D_MODEL), BF16)],
        scratch_shapes=[pltpu.VMEM((rows, 2 * WIDTH), BF16),
                        pltpu.VMEM((rows, 2 * WIDTH), BF16),
                        pltpu.VMEM((A_GROUPS, CHUNK, CHUNK), BF16),
                        pltpu.VMEM((SUBLANES + rows, B_XBC), F32),
                        pltpu.VMEM((CHUNK, WIDTH), F32),
                        pltpu.VMEM((CHUNK, B_GROUPS * B_STATE), F32),
                        pltpu.VMEM((CHUNK, B_GROUPS * B_STATE), F32),
                        pltpu.VMEM((CHUNK, WIDTH), F32),
                        pltpu.VMEM((B_HEADS, B_STATE, B_HEAD_DIM), F32)],
        compiler_params=_cparams("arbitrary", "arbitrary"),
    )(h0, h0, h0, h0, h0, dt_raw, x0, w_out, row(next_norm_g), *params)


def _cast_kernel(w_ref, o_ref):
    o_ref[...] = w_ref[...].astype(o_ref.dtype)


def _cast_bf16(w, tr=512):
    r, c = w.shape
    return pl.pallas_call(
        _cast_kernel,
        grid=(r // tr,),
        in_specs=[pl.BlockSpec((tr, c), lambda i: (i, 0))],
        out_specs=pl.BlockSpec((tr, c), lambda i: (i, 0)),
        out_shape=jax.ShapeDtypeStruct((r, c), BF16),
        compiler_params=_cparams("arbitrary"),
    )(w)


def _shortconv_kernel(bg_ref, cg_ref, hx_ref, zc_ref, cw_ref, o_ref, xbuf_ref):
    tile = o_ref.shape[0]

    @pl.when(pl.program_id(1) == 0)
    def _():
        xbuf_ref[0:SUBLANES, :] = jnp.zeros((SUBLANES, WIDTH), F32)

    cwid, rows = 256, 128
    for j in range(WIDTH // cwid):
        sl = slice(j * cwid, (j + 1) * cwid)
        for i in range(tile // rows):
            r0 = i * rows
            xbuf_ref[SUBLANES + r0:SUBLANES + r0 + rows, sl] = (
                cg_ref[r0:r0 + rows, sl].astype(F32) * hx_ref[r0:r0 + rows, sl].astype(F32))
        for i in range(tile // rows):
            r0 = i * rows
            acc = None
            for k in range(C_CONV):
                lo = SUBLANES - (C_CONV - 1) + k + r0
                term = xbuf_ref[lo:lo + rows, sl] * cw_ref[k:k + 1, sl]
                acc = term if acc is None else acc + term
            gate = _silu(zc_ref[r0:r0 + rows, sl].astype(F32)) * bg_ref[r0:r0 + rows, sl].astype(F32)
            o_ref[r0:r0 + rows, sl] = (gate * acc).astype(o_ref.dtype)
        xbuf_ref[0:SUBLANES, sl] = xbuf_ref[tile:tile + SUBLANES, sl]


def _shortconv(h1, conv_w, batch, seq, tile=512):
    t = batch * seq
    nt = seq // tile
    tok = lambda cb: (lambda b, i: (b * nt + i, cb))
    return pl.pallas_call(
        _shortconv_kernel,
        grid=(batch, nt),
        in_specs=[pl.BlockSpec((tile, WIDTH), tok(0)), pl.BlockSpec((tile, WIDTH), tok(1)),
                  pl.BlockSpec((tile, WIDTH), tok(2)), pl.BlockSpec((tile, WIDTH), tok(3)),
                  pl.BlockSpec(conv_w.shape, lambda b, i: (0, 0))],
        out_specs=pl.BlockSpec((tile, WIDTH), tok(0)),
        out_shape=jax.ShapeDtypeStruct((t, WIDTH), BF16),
        scratch_shapes=[pltpu.VMEM((SUBLANES + tile, WIDTH), F32)],
        compiler_params=_cparams("arbitrary", "arbitrary"),
    )(h1, h1, h1, h1, conv_w)


R4 = DILATIONS[1]
ROWS4 = SUPER // R4


def _attn4_kernel(q_ref, kp_ref, kc_ref, vp_ref, vc_ref, zd_ref, o_ref,
                  qn_ref, kn_ref, vn_ref, q4_ref, k4_ref, v4_ref, os_ref, ls_ref, ms_ref, fin_ref, bias_ref):
    first_super = pl.program_id(2) == 0
    nq = N_BACK
    blocks = SUPER // nq
    a = lax.broadcasted_iota(jnp.int32, (nq, 2 * nq), 0)
    j = lax.broadcasted_iota(jnp.int32, (nq, 2 * nq), 1)
    in_band = jnp.logical_and(j >= a, j <= a + nq)
    bias_ref[0] = jnp.where(in_band, 0.0, -jnp.inf)
    bias_ref[1] = jnp.where(jnp.logical_and(in_band, j >= nq), 0.0, -jnp.inf)
    qscale = D_HEAD_DIM ** -0.5 * 1.4426950408889634
    ones_cols = jnp.ones((2 * nq, LANES), BF16)

    qn_ref[...] = q_ref[...].astype(F32) * qscale
    kn_ref[0:SUPER] = kp_ref[...].astype(F32)
    kn_ref[SUPER:] = kc_ref[...].astype(F32)
    vn_ref[0:SUPER] = vp_ref[...].astype(F32)
    vn_ref[SUPER:] = vc_ref[...].astype(F32)
    for r in range(R4):
        q4_ref[r * ROWS4:(r + 1) * ROWS4] = qn_ref[pl.ds(r, ROWS4, stride=R4), :]
        for half in range(2):
            dst = slice(half * SUPER + r * ROWS4, half * SUPER + (r + 1) * ROWS4)
            k4_ref[dst] = kn_ref[pl.ds(half * SUPER + r, ROWS4, stride=R4), :]
            v4_ref[dst] = vn_ref[pl.ds(half * SUPER + r, ROWS4, stride=R4), :]

    def softmax_block(qf, kf, vf, no_prev):
        s = lax.dot_general(qf.astype(BF16), kf.astype(BF16), (((1,), (1,)), ((), ())),
                            preferred_element_type=F32) + bias_ref[no_prev.astype(jnp.int32)]
        m = jnp.max(s, axis=-1, keepdims=True)
        p = jnp.exp2(s - m).astype(BF16)
        o2 = jnp.dot(p, jnp.concatenate([vf.astype(BF16), ones_cols], axis=1), preferred_element_type=F32)
        return o2[:, :LANES], o2[:, LANES:], jnp.broadcast_to(m, (nq, LANES))

    def block(i, carry):
        q1 = pl.ds(pl.multiple_of(i * nq, nq), nq)
        k1 = pl.ds(pl.multiple_of(SUPER + (i - 1) * nq, nq), 2 * nq)
        in1 = (qn_ref[q1, :], kn_ref[k1, :], vn_ref[k1, :], jnp.logical_and(first_super, i == 0))
        res, sub = i // (ROWS4 // nq), i % (ROWS4 // nq)
        q4 = pl.ds(pl.multiple_of(res * ROWS4 + sub * nq, nq), nq)
        kc4 = pl.ds(pl.multiple_of(SUPER + res * ROWS4 + sub * nq, nq), nq)
        kp4 = pl.ds(pl.multiple_of(jnp.where(sub > 0, SUPER + res * ROWS4 + (sub - 1) * nq,
                                             res * ROWS4 + ROWS4 - nq), nq), nq)
        in4 = (q4_ref[q4, :],
               jnp.concatenate([k4_ref[kp4, :], k4_ref[kc4, :]], axis=0),
               jnp.concatenate([v4_ref[kp4, :], v4_ref[kc4, :]], axis=0),
               jnp.logical_and(first_super, sub == 0))
        lo, hi = i % R4, i // R4
        q16 = pl.ds(lo * ROWS4 + hi, nq, stride=R4)
        kp16 = pl.ds(lo * ROWS4 + hi, nq, stride=R4)
        kc16 = pl.ds(SUPER + lo * ROWS4 + hi, nq, stride=R4)
        in16 = (q4_ref[q16, :],
                jnp.concatenate([k4_ref[kp16, :], k4_ref[kc16, :]], axis=0),
                jnp.concatenate([v4_ref[kp16, :], v4_ref[kc16, :]], axis=0),
                first_super)
        outs = [softmax_block(*args) for args in (in1, in4, in16)]
        for pi, qsl in enumerate((q1, q4, q16)):
            os_ref[pi, qsl, :] = outs[pi][0]
            ls_ref[pi, qsl, :] = outs[pi][1]
            ms_ref[pi, qsl, :] = outs[pi][2]
        return carry

    lax.fori_loop(0, blocks, block, 0, unroll=4)

    rows = 256

    def combine(c, carry):
        home = pl.ds(pl.multiple_of(c * rows, rows), rows)
        r, part = c // (ROWS4 // rows), c % (ROWS4 // rows)
        tok = pl.ds(r + part * (rows * R4), rows, stride=R4)
        ms = [ms_ref[0, tok, :], ms_ref[1, home, :], ms_ref[2, home, :]]
        os = [os_ref[0, tok, :], os_ref[1, home, :], os_ref[2, home, :]]
        ls = [ls_ref[0, tok, :], ls_ref[1, home, :], ls_ref[2, home, :]]
        mx = jnp.maximum(jnp.maximum(ms[0], ms[1]), ms[2])
        num = jnp.zeros((rows, LANES), F32)
        den = jnp.zeros((rows, LANES), F32)
        for pi in range(len(DILATIONS)):
            e = jnp.exp2(ms[pi] - mx)
            num = num + e * os[pi]
            den = den + e * ls[pi]
        fin_ref[tok, :] = num / den
        return carry

    lax.fori_loop(0, SUPER // rows, combine, 0)

    def gate(c, carry):
        rs = pl.ds(pl.multiple_of(c * rows, rows), rows)
        o_ref[rs, :] = (_silu(zd_ref[rs, :].astype(F32)) * fin_ref[rs, :]).astype(o_ref.dtype)
        return carry

    lax.fori_loop(0, SUPER // rows, gate, 0)


def _attention4(h1, col0, batch, seq):
    t = batch * seq
    ns = seq // SUPER
    hw = D_HEAD_DIM
    cb0 = col0 // hw
    per = WIDTH // hw
    npat = len(DILATIONS)

    def cur(part):
        return lambda b, g, s: (b * ns + s, cb0 + part * per + g)

    def prev(part):
        return lambda b, g, s: (b * ns + jnp.maximum(s - 1, 0), cb0 + part * per + g)

    blk = (SUPER, hw)
    return pl.pallas_call(
        _attn4_kernel,
        grid=(batch, per, ns),
        in_specs=[pl.BlockSpec(blk, cur(0)),
                  pl.BlockSpec(blk, prev(1)), pl.BlockSpec(blk, cur(1)),
                  pl.BlockSpec(blk, prev(2)), pl.BlockSpec(blk, cur(2)),
                  pl.BlockSpec(blk, cur(3))],
        out_specs=pl.BlockSpec(blk, lambda b, g, s: (b * ns + s, g)),
        out_shape=jax.ShapeDtypeStruct((t, WIDTH), BF16),
        scratch_shapes=[pltpu.VMEM((SUPER, LANES), F32),
                        pltpu.VMEM((2 * SUPER, LANES), F32),
                        pltpu.VMEM((2 * SUPER, LANES), F32),
                        pltpu.VMEM((SUPER, LANES), F32),
                        pltpu.VMEM((2 * SUPER, LANES), F32),
                        pltpu.VMEM((2 * SUPER, LANES), F32),
                        pltpu.VMEM((npat, SUPER, LANES), F32),
                        pltpu.VMEM((npat, SUPER, LANES), F32),
                        pltpu.VMEM((npat, SUPER, LANES), F32),
                        pltpu.VMEM((SUPER, LANES), F32),
                        pltpu.VMEM((2, N_BACK, 2 * N_BACK), F32)],
        compiler_params=_cparams("arbitrary", "arbitrary", "arbitrary"),
    )(h1, h1, h1, h1, h1, h1)


def kernel(x, even_norm_g, even_w_in, gmlp_ln_g, gmlp_ln_b, gmlp_ws, gmlp_bs, ssd_conv_w, ssd_conv_b,
           ssd_dt_bias, ssd_a_log, ssd_d, ssd_norm_g, even_w_out, odd_norm_g, odd_w_in, sconv_w,
           odd_w_out, final_norm_g):
    batch, seq, d = x.shape
    assert d == D_MODEL and seq % SUPER == 0
    assert even_norm_g.shape[0] == 1 and odd_norm_g.shape[0] == 1
    t = batch * seq
    x0 = x.reshape(t, d)
    w_in0_t = jnp.swapaxes(even_w_in, 1, 2).reshape(-1, d)
    w_out0 = even_w_out.reshape(-1, d)
    w_in1 = odd_w_in.reshape(d, -1)
    w_out1 = odd_w_out.reshape(-1, d)

    n_main = 4 * WIDTH + B_XBC
    w_dt = jnp.pad(w_in0_t[n_main:, :].T, ((0, 0), (0, LANES - B_HEADS)))
    xn0, dt_raw = _rmsnorm_dt(x0, even_norm_g[0], w_dt)
    h0 = _matmul(xn0, w_in0_t, n_main, w_is_nk=True)
    x1, xn1 = _mix0(h0, dt_raw, x0, _cast_bf16(w_out0), odd_norm_g[0], batch, seq,
                    gmlp_ln_g[0], gmlp_ln_b[0], gmlp_ws[0], gmlp_bs[0],
                    ssd_conv_w[0], ssd_conv_b[0], ssd_dt_bias[0], ssd_a_log[0], ssd_d[0], ssd_norm_g[0])

    h1 = _matmul(xn1, w_in1, w_in1.shape[1])
    yc = _shortconv(h1, sconv_w[0], batch, seq)
    yd = _attention4(h1, 4 * WIDTH, batch, seq)
    return _outproj_norm(yc, yd, _cast_bf16(w_out1), x1, final_norm_g).reshape(batch, seq, d)
```

```python
import functools

import jax
import jax.numpy as jnp
from jax import lax
from jax.experimental import pallas as pl
from jax.experimental.pallas import tpu as pltpu

F32 = jnp.float32
BF16 = jnp.bfloat16

EPS = 1e-5
D_MODEL = 2048
WIDTH = 2048
A_GROUPS = 8
CHUNK = 128
MIX0_CHUNKS = 1
B_HEAD_DIM = 64
B_HEADS = WIDTH // B_HEAD_DIM
B_GROUPS = 8
B_STATE = 128
B_CONV = 4
B_XBC = WIDTH + 2 * B_GROUPS * B_STATE
HEADS_PER_GROUP = B_HEADS // B_GROUPS
GROUP_W = WIDTH // B_GROUPS
C_CONV = 3
D_HEAD_DIM = 128
D_HEADS = WIDTH // D_HEAD_DIM
N_BACK = 128
DILATIONS = (1, 4, 16)
SUPER = N_BACK * DILATIONS[-1]
LANES = 128
SUBLANES = 8
VMEM_LIMIT = 56 * 1024 * 1024


def _cparams(*sem):
    return pltpu.CompilerParams(dimension_semantics=sem, vmem_limit_bytes=VMEM_LIMIT)


def _silu(z):
    hz = 0.5 * z
    return hz * jnp.tanh(hz) + hz


def _rmsnorm_dt_kernel(x_ref, g_ref, w_ref, o_ref, dt_ref):
    x = x_ref[...]
    ms = jnp.mean(x * x, axis=-1, keepdims=True)
    xn = (x * lax.rsqrt(ms + EPS) * g_ref[...]).astype(o_ref.dtype)
    o_ref[...] = xn
    dt_ref[...] = jnp.dot(xn, w_ref[...].astype(BF16), preferred_element_type=F32)


def _rmsnorm_dt(x2d, g, w_dt, tm=512):
    t, d = x2d.shape
    n = w_dt.shape[1]
    return pl.pallas_call(
        _rmsnorm_dt_kernel,
        grid=(t // tm,),
        in_specs=[pl.BlockSpec((tm, d), lambda i: (i, 0)),
                  pl.BlockSpec((1, d), lambda i: (0, 0)),
                  pl.BlockSpec((d, n), lambda i: (0, 0))],
        out_specs=[pl.BlockSpec((tm, d), lambda i: (i, 0)), pl.BlockSpec((tm, n), lambda i: (i, 0))],
        out_shape=[jax.ShapeDtypeStruct((t, d), BF16), jax.ShapeDtypeStruct((t, n), F32)],
        compiler_params=_cparams("arbitrary"),
    )(x2d, g.reshape(1, d), w_dt)


def _outproj_norm_kernel(a1_ref, a2_ref, w_ref, r_ref, g_ref, o_ref):
    k1 = a1_ref.shape[1]
    x = (jnp.dot(a1_ref[...], w_ref[0:k1, :], preferred_element_type=F32)
         + jnp.dot(a2_ref[...], w_ref[k1:, :], preferred_element_type=F32) + r_ref[...])
    ms = jnp.mean(x * x, axis=-1, keepdims=True)
    o_ref[...] = x * lax.rsqrt(ms + EPS) * g_ref[...]


def _outproj_norm(a1, a2, w_bf16, res, g, tm=512):
    m, k1 = a1.shape
    k2 = a2.shape[1]
    n = w_bf16.shape[1]
    return pl.pallas_call(
        _outproj_norm_kernel,
        grid=(m // tm,),
        in_specs=[pl.BlockSpec((tm, k1), lambda i: (i, 0)),
                  pl.BlockSpec((tm, k2), lambda i: (i, 0)),
                  pl.BlockSpec(w_bf16.shape, lambda i: (0, 0), pipeline_mode=pl.Buffered(1)),
                  pl.BlockSpec((tm, n), lambda i: (i, 0)),
                  pl.BlockSpec((1, n), lambda i: (0, 0))],
        out_specs=pl.BlockSpec((tm, n), lambda i: (i, 0)),
        out_shape=jax.ShapeDtypeStruct((m, n), F32),
        compiler_params=_cparams("arbitrary"),
    )(a1, a2, w_bf16, res, g.reshape(1, n))


def _matmul_kernel(a_ref, w_ref, o_ref, wb_ref, *, cast_rows, w_is_nk, silu_blocks):
    @pl.when(pl.program_id(1) == 0)
    def _():
        def body(k, carry):
            r = pl.multiple_of(k * cast_rows, cast_rows)
            wb_ref[pl.ds(r, cast_rows), :] = w_ref[pl.ds(r, cast_rows), :].astype(BF16)
            return carry
        lax.fori_loop(0, w_ref.shape[0] // cast_rows, body, 0)

    dims = (((1,), (1,)), ((), ())) if w_is_nk else (((1,), (0,)), ((), ()))
    j = pl.program_id(0)
    gated = functools.reduce(jnp.logical_or, [jnp.logical_and(j >= lo, j < hi) for lo, hi in silu_blocks])

    @pl.when(gated)
    def _():
        acc = lax.dot_general(a_ref[...], wb_ref[...], dims, preferred_element_type=F32)
        o_ref[...] = _silu(acc).astype(o_ref.dtype)

    @pl.when(jnp.logical_not(gated))
    def _():
        o_ref[...] = lax.dot_general(a_ref[...], wb_ref[...], dims, preferred_element_type=F32).astype(o_ref.dtype)


def _matmul(a, w, n_cols, silu_cols, w_is_nk=False, tm=1024, tn=1024):
    m, k = a.shape
    assert all(lo % tn == 0 and hi % tn == 0 for lo, hi in silu_cols)
    silu_blocks = tuple((lo // tn, hi // tn) for lo, hi in silu_cols)
    if w_is_nk:
        w_block = pl.BlockSpec((tn, k), lambda j, i: (j, 0))
    else:
        w_block = pl.BlockSpec((k, tn), lambda j, i: (0, j))
    return pl.pallas_call(
        functools.partial(_matmul_kernel, cast_rows=256, w_is_nk=w_is_nk, silu_blocks=silu_blocks),
        grid=(n_cols // tn, m // tm),
        in_specs=[pl.BlockSpec((tm, k), lambda j, i: (i, 0)), w_block],
        out_specs=pl.BlockSpec((tm, tn), lambda j, i: (i, j)),
        out_shape=jax.ShapeDtypeStruct((m, n_cols), BF16),
        scratch_shapes=[pltpu.VMEM(w_block.block_shape, BF16)],
        compiler_params=_cparams("arbitrary", "arbitrary"),
    )(a, w)


def _softplus(x):
    return jnp.maximum(x, 0.0) + jnp.log1p(jnp.exp(-jnp.abs(x)))


def _split3(x):
    hi = x.astype(BF16)
    r1 = x - hi.astype(F32)
    mid = r1.astype(BF16)
    lo = (r1 - mid.astype(F32)).astype(BF16)
    return hi, mid, lo


def _mix0_kernel(u_ref, v_ref, za_ref, zb_ref, xbc_ref, dt_ref, xres_ref, wout_ref, g1_ref,
                 lng_ref, lnb_ref, ws_ref, bst_ref, cw_ref, cbias_ref, dtb_ref, alog_ref, dexp_ref, ng_ref,
                 x1_ref, xn1_ref,
                 o_ref, yprev_ref, wsb_ref, xbuf_ref, xs_ref, bm_ref, cm_ref, y_ref, st_ref):
    q = CHUNK
    row = lax.broadcasted_iota(jnp.int32, (q, q), 0)
    col = lax.broadcasted_iota(jnp.int32, (q, q), 1)
    causal = col <= row

    @pl.when(pl.program_id(1) == 0)
    def _():
        o_ref[...] = jnp.zeros_like(o_ref)
        xbuf_ref[0:SUBLANES, :] = jnp.zeros((SUBLANES, B_XBC), F32)
        st_ref[...] = jnp.zeros_like(st_ref)
        for g in range(A_GROUPS):
            wsb_ref[g] = jnp.where(causal, ws_ref[g], 0.0).astype(BF16)

    yprev_ref[...] = o_ref[...]
    ncol = 512
    ss = jnp.zeros((o_ref.shape[0], 1), F32)
    for nb in range(D_MODEL // ncol):
        cs = slice(nb * ncol, (nb + 1) * ncol)
        xc = jnp.dot(yprev_ref[...], wout_ref[:, cs], preferred_element_type=F32) + xres_ref[:, cs]
        x1_ref[:, cs] = xc
        ss = ss + jnp.sum(xc * xc, axis=-1, keepdims=True)
    rstd1 = lax.rsqrt(ss * (1.0 / D_MODEL) + EPS)
    for nb in range(D_MODEL // ncol):
        cs = slice(nb * ncol, (nb + 1) * ncol)
        xn1_ref[:, cs] = (x1_ref[:, cs] * rstd1 * g1_ref[:, cs]).astype(xn1_ref.dtype)

    tril = jnp.where(causal, 1.0, 0.0).astype(BF16)
    for ci in range(o_ref.shape[0] // q):
        rs = slice(ci * q, (ci + 1) * q)
        _mix0_chunk(rs, causal, tril, u_ref, v_ref, za_ref, zb_ref, xbc_ref, dt_ref,
                    lng_ref, lnb_ref, bst_ref, cw_ref, cbias_ref, dtb_ref, alog_ref, dexp_ref, ng_ref,
                    o_ref, wsb_ref, xbuf_ref, xs_ref, bm_ref, cm_ref, y_ref, st_ref)
    rows = o_ref.shape[0]
    xbuf_ref[0:SUBLANES, :] = xbuf_ref[rows:rows + SUBLANES, :]


def _mix0_chunk(rs, causal, tril, u_ref, v_ref, za_ref, zb_ref, xbc_ref, dt_ref,
                lng_ref, lnb_ref, bst_ref, cw_ref, cbias_ref, dtb_ref, alog_ref, dexp_ref, ng_ref,
                o_ref, wsb_ref, xbuf_ref, xs_ref, bm_ref, cm_ref, y_ref, st_ref):
    q = CHUNK
    v = v_ref[rs, :].astype(F32)
    mu = jnp.mean(v, axis=-1, keepdims=True)
    xc = v - mu
    var = jnp.mean(xc * xc, axis=-1, keepdims=True)
    rstd = lax.rsqrt(var + EPS)
    for g in range(A_GROUPS):
        sl = slice(g * GROUP_W, (g + 1) * GROUP_W)
        vg = v_ref[rs, sl].astype(F32)
        vn = ((vg - mu) * rstd * lng_ref[:, sl] + lnb_ref[:, sl]).astype(BF16)
        mixed = jnp.dot(wsb_ref[g], vn, preferred_element_type=F32) + bst_ref[:, g:g + 1]
        gate = za_ref[rs, sl].astype(F32)
        u = u_ref[rs, sl].astype(F32)
        o_ref[rs, sl] = (gate * (u * mixed)).astype(o_ref.dtype)

    cwid = 256
    base = SUBLANES + rs.start
    for j in range(B_XBC // cwid):
        sl = slice(j * cwid, (j + 1) * cwid)
        xbuf_ref[base:base + q, sl] = xbc_ref[rs, sl].astype(F32)
        acc = cbias_ref[:, sl]
        for k in range(B_CONV):
            lo = base - (B_CONV - 1) + k
            acc = acc + xbuf_ref[lo:lo + q, sl] * cw_ref[k:k + 1, sl]
        act = _silu(acc)
        lo = j * cwid
        if lo < WIDTH:
            xs_ref[:, lo:lo + cwid] = act
        elif lo < WIDTH + B_GROUPS * B_STATE:
            bm_ref[:, lo - WIDTH:lo - WIDTH + cwid] = act
        else:
            off = lo - WIDTH - B_GROUPS * B_STATE
            cm_ref[:, off:off + cwid] = act

    dt = _softplus(dt_ref[rs, :] + dtb_ref[...])
    adt = dt * (-jnp.exp(alog_ref[...]))
    a_cs = sum(jnp.dot(tril, part, preferred_element_type=F32) for part in _split3(adt))
    a_cs_t = a_cs.T
    dt_t = dt.T

    for g in range(B_GROUPS):
        gs = slice(g * B_STATE, (g + 1) * B_STATE)
        bg = bm_ref[:, gs]
        cgb = cm_ref[:, gs].astype(BF16)
        bg_t = bg.T
        cb = lax.dot_general(cgb, bg.astype(BF16), (((1,), (1,)), ((), ())),
                             preferred_element_type=F32)
        for r in range(HEADS_PER_GROUP):
            h = g * HEADS_PER_GROUP + r
            hs = slice(h * B_HEAD_DIM, (h + 1) * B_HEAD_DIM)
            colv = a_cs[:, h:h + 1]
            rowv = a_cs_t[h:h + 1, :]
            dtr = dt_t[h:h + 1, :]
            decay = jnp.exp(jnp.where(causal, colv - rowv, -jnp.inf))
            mh = (cb * decay * dtr).astype(BF16)
            xh = xs_ref[:, hs]
            xhb = xh.astype(BF16)
            state = st_ref[h]
            y = jnp.dot(mh, xhb, preferred_element_type=F32)
            y = y + jnp.exp(colv) * jnp.dot(cgb, state.astype(BF16), preferred_element_type=F32)
            y_ref[:, hs] = y + dexp_ref[:, hs] * xh
            last = a_cs_t[h:h + 1, q - 1:q]
            wrow = jnp.exp(last - rowv) * dtr
            new = jnp.dot((bg_t * wrow).astype(BF16), xhb, preferred_element_type=F32)
            st_ref[h] = state * jnp.exp(last) + new

    for g in range(B_GROUPS):
        sl = slice(g * GROUP_W, (g + 1) * GROUP_W)
        y = y_ref[:, sl] * zb_ref[rs, sl].astype(F32)
        ms = jnp.mean(y * y, axis=-1, keepdims=True)
        o_ref[rs, WIDTH + g * GROUP_W:WIDTH + (g + 1) * GROUP_W] = (
            y * lax.rsqrt(ms + EPS) * ng_ref[:, sl]).astype(o_ref.dtype)


def _mix0(h0, dt_raw, x0, w_out, next_norm_g, batch, seq,
          lng, lnb, ws, bs, conv_w, conv_b, dt_bias, a_log, d_skip, norm_g):
    t = batch * seq
    rows = MIX0_CHUNKS * CHUNK
    nc = seq // rows
    pad = LANES - B_HEADS
    row = lambda a: a.reshape(1, -1)
    params = [row(lng), row(lnb), ws, bs.T, conv_w, row(conv_b),
              row(jnp.pad(dt_bias, (0, pad))), row(jnp.pad(a_log, (0, pad))),
              row(jnp.repeat(d_skip, B_HEAD_DIM)), row(norm_g)]
    tok = lambda cb: (lambda b, c: (b * nc + jnp.minimum(c, nc - 1), cb))
    lag = lambda b, c: (b * nc + jnp.maximum(c - 1, 0), 0)
    full = lambda a: pl.BlockSpec(a.shape, lambda b, c: (0,) * a.ndim)
    in_specs = [pl.BlockSpec((rows, WIDTH), tok(0)),
                pl.BlockSpec((rows, WIDTH), tok(1)),
                pl.BlockSpec((rows, WIDTH), tok(2)),
                pl.BlockSpec((rows, WIDTH), tok(3)),
                pl.BlockSpec((rows, B_XBC), tok(2)),
                pl.BlockSpec((rows, LANES), tok(0)),
                pl.BlockSpec((rows, D_MODEL), lag),
                pl.BlockSpec(w_out.shape, lambda b, c: (0, 0), pipeline_mode=pl.Buffered(1)),
                full(row(next_norm_g))]
    in_specs += [full(p) for p in params]
    return pl.pallas_call(
        _mix0_kernel,
        grid=(batch, nc + 1),
        in_specs=in_specs,
        out_specs=[pl.BlockSpec((rows, D_MODEL), lag), pl.BlockSpec((rows, D_MODEL), lag)],
        out_shape=[jax.ShapeDtypeStruct((t, D_MODEL), F32), jax.ShapeDtypeStruct((t, D_MODEL), BF16)],
        scratch_shapes=[pltpu.VMEM((rows, 2 * WIDTH), BF16),
                        pltpu.VMEM((rows, 2 * WIDTH), BF16),
                        pltpu.VMEM((A_GROUPS, CHUNK, CHUNK), BF16),
                        pltpu.VMEM((SUBLANES + rows, B_XBC), F32),
                        pltpu.VMEM((CHUNK, WIDTH), F32),
                        pltpu.VMEM((CHUNK, B_GROUPS * B_STATE), F32),
                        pltpu.VMEM((CHUNK, B_GROUPS * B_STATE), F32),
                        pltpu.VMEM((CHUNK, WIDTH), F32),
                        pltpu.VMEM((B_HEADS, B_STATE, B_HEAD_DIM), F32)],
        compiler_params=_cparams("arbitrary", "arbitrary"),
    )(h0, h0, h0, h0, h0, dt_raw, x0, w_out, row(next_norm_g), *params)


def _cast_kernel(w_ref, o_ref):
    o_ref[...] = w_ref[...].astype(o_ref.dtype)


def _cast_bf16(w, tr=512):
    r, c = w.shape
    return pl.pallas_call(
        _cast_kernel,
        grid=(r // tr,),
        in_specs=[pl.BlockSpec((tr, c), lambda i: (i, 0))],
        out_specs=pl.BlockSpec((tr, c), lambda i: (i, 0)),
        out_shape=jax.ShapeDtypeStruct((r, c), BF16),
        compiler_params=_cparams("arbitrary"),
    )(w)


def _shortconv_kernel(bg_ref, cg_ref, hx_ref, zc_ref, cw_ref, o_ref, xbuf_ref):
    tile = o_ref.shape[0]

    @pl.when(pl.program_id(1) == 0)
    def _():
        xbuf_ref[0:SUBLANES, :] = jnp.zeros((SUBLANES, WIDTH), F32)

    cwid, rows = 256, 128
    for j in range(WIDTH // cwid):
        sl = slice(j * cwid, (j + 1) * cwid)
        for i in range(tile // rows):
            r0 = i * rows
            xbuf_ref[SUBLANES + r0:SUBLANES + r0 + rows, sl] = (
                cg_ref[r0:r0 + rows, sl].astype(F32) * hx_ref[r0:r0 + rows, sl].astype(F32))
        for i in range(tile // rows):
            r0 = i * rows
            acc = None
            for k in range(C_CONV):
                lo = SUBLANES - (C_CONV - 1) + k + r0
                term = xbuf_ref[lo:lo + rows, sl] * cw_ref[k:k + 1, sl]
                acc = term if acc is None else acc + term
            gate = zc_ref[r0:r0 + rows, sl].astype(F32) * bg_ref[r0:r0 + rows, sl].astype(F32)
            o_ref[r0:r0 + rows, sl] = (gate * acc).astype(o_ref.dtype)
        xbuf_ref[0:SUBLANES, sl] = xbuf_ref[tile:tile + SUBLANES, sl]


def _shortconv(h1, conv_w, batch, seq, tile=512):
    t = batch * seq
    nt = seq // tile
    tok = lambda cb: (lambda b, i: (b * nt + i, cb))
    return pl.pallas_call(
        _shortconv_kernel,
        grid=(batch, nt),
        in_specs=[pl.BlockSpec((tile, WIDTH), tok(0)), pl.BlockSpec((tile, WIDTH), tok(1)),
                  pl.BlockSpec((tile, WIDTH), tok(2)), pl.BlockSpec((tile, WIDTH), tok(3)),
                  pl.BlockSpec(conv_w.shape, lambda b, i: (0, 0))],
        out_specs=pl.BlockSpec((tile, WIDTH), tok(0)),
        out_shape=jax.ShapeDtypeStruct((t, WIDTH), BF16),
        scratch_shapes=[pltpu.VMEM((SUBLANES + tile, WIDTH), F32)],
        compiler_params=_cparams("arbitrary", "arbitrary"),
    )(h1, h1, h1, h1, conv_w)


R4 = DILATIONS[1]
ROWS4 = SUPER // R4


def _attn4_kernel(q_ref, kc_ref, vc_ref, zd_ref, o_ref,
                  qn_ref, kn_ref, vn_ref, q4_ref, k4_ref, v4_ref, os_ref, ls_ref, ms_ref, fin_ref, bias_ref):
    first_super = pl.program_id(2) == 0
    cur0 = pl.multiple_of((pl.program_id(2) % 2) * SUPER, SUPER)
    prev0 = pl.multiple_of(SUPER - cur0, SUPER)
    nq = N_BACK
    blocks = SUPER // nq
    a = lax.broadcasted_iota(jnp.int32, (nq, 2 * nq), 0)
    j = lax.broadcasted_iota(jnp.int32, (nq, 2 * nq), 1)
    in_band = jnp.logical_and(j >= a, j <= a + nq)
    bias_ref[0] = jnp.where(in_band, 0.0, -jnp.inf)
    bias_ref[1] = jnp.where(jnp.logical_and(in_band, j >= nq), 0.0, -jnp.inf)
    qscale = D_HEAD_DIM ** -0.5 * 1.4426950408889634
    ones_cols = jnp.ones((2 * nq, LANES), BF16)

    @pl.when(first_super)
    def _():
        for ref in (kn_ref, vn_ref, k4_ref, v4_ref):
            ref[pl.ds(prev0, SUPER), :] = jnp.zeros((SUPER, LANES), F32)

    qn_ref[...] = q_ref[...].astype(F32) * qscale
    kn_ref[pl.ds(cur0, SUPER), :] = kc_ref[...].astype(F32)
    vn_ref[pl.ds(cur0, SUPER), :] = vc_ref[...].astype(F32)
    for r in range(R4):
        q4_ref[r * ROWS4:(r + 1) * ROWS4] = qn_ref[pl.ds(r, ROWS4, stride=R4), :]
        dst = pl.ds(pl.multiple_of(cur0 + r * ROWS4, ROWS4), ROWS4)
        k4_ref[dst, :] = kn_ref[pl.ds(cur0 + r, ROWS4, stride=R4), :]
        v4_ref[dst, :] = vn_ref[pl.ds(cur0 + r, ROWS4, stride=R4), :]

    def softmax_block(qf, kf, vf, no_prev):
        s = lax.dot_general(qf.astype(BF16), kf.astype(BF16), (((1,), (1,)), ((), ())),
                            preferred_element_type=F32) + bias_ref[no_prev.astype(jnp.int32)]
        m = jnp.max(s, axis=-1, keepdims=True)
        p = jnp.exp2(s - m).astype(BF16)
        o2 = jnp.dot(p, jnp.concatenate([vf.astype(BF16), ones_cols], axis=1), preferred_element_type=F32)
        return o2[:, :LANES], o2[:, LANES:], jnp.broadcast_to(m, (nq, LANES))

    def block(i, carry):
        q1 = pl.ds(pl.multiple_of(i * nq, nq), nq)
        kc1 = pl.ds(pl.multiple_of(cur0 + i * nq, nq), nq)
        kp1 = pl.ds(pl.multiple_of(jnp.where(i > 0, cur0 + (i - 1) * nq, prev0 + SUPER - nq), nq), nq)
        in1 = (qn_ref[q1, :],
               jnp.concatenate([kn_ref[kp1, :], kn_ref[kc1, :]], axis=0),
               jnp.concatenate([vn_ref[kp1, :], vn_ref[kc1, :]], axis=0),
               jnp.logical_and(first_super, i == 0))
        res, sub = i // (ROWS4 // nq), i % (ROWS4 // nq)
        q4 = pl.ds(pl.multiple_of(res * ROWS4 + sub * nq, nq), nq)
        kc4 = pl.ds(pl.multiple_of(cur0 + res * ROWS4 + sub * nq, nq), nq)
        kp4 = pl.ds(pl.multiple_of(jnp.where(sub > 0, cur0 + res * ROWS4 + (sub - 1) * nq,
                                             prev0 + res * ROWS4 + ROWS4 - nq), nq), nq)
        in4 = (q4_ref[q4, :],
               jnp.concatenate([k4_ref[kp4, :], k4_ref[kc4, :]], axis=0),
               jnp.concatenate([v4_ref[kp4, :], v4_ref[kc4, :]], axis=0),
               jnp.logical_and(first_super, sub == 0))
        lo, hi = i % R4, i // R4
        q16 = pl.ds(lo * ROWS4 + hi, nq, stride=R4)
        kp16 = pl.ds(prev0 + lo * ROWS4 + hi, nq, stride=R4)
        kc16 = pl.ds(cur0 + lo * ROWS4 + hi, nq, stride=R4)
        in16 = (q4_ref[q16, :],
                jnp.concatenate([k4_ref[kp16, :], k4_ref[kc16, :]], axis=0),
                jnp.concatenate([v4_ref[kp16, :], v4_ref[kc16, :]], axis=0),
                first_super)
        outs = [softmax_block(*args) for args in (in1, in4, in16)]
        for pi, qsl in enumerate((q1, q4, q16)):
            os_ref[pi, qsl, :] = outs[pi][0]
            ls_ref[pi, qsl, :] = outs[pi][1]
            ms_ref[pi, qsl, :] = outs[pi][2]
        return carry

    lax.fori_loop(0, blocks, block, 0, unroll=4)

    rows = 256

    def combine(c, carry):
        home = pl.ds(pl.multiple_of(c * rows, rows), rows)
        r, part = c // (ROWS4 // rows), c % (ROWS4 // rows)
        tok = pl.ds(r + part * (rows * R4), rows, stride=R4)
        ms = [ms_ref[0, tok, :], ms_ref[1, home, :], ms_ref[2, home, :]]
        os = [os_ref[0, tok, :], os_ref[1, home, :], os_ref[2, home, :]]
        ls = [ls_ref[0, tok, :], ls_ref[1, home, :], ls_ref[2, home, :]]
        mx = jnp.maximum(jnp.maximum(ms[0], ms[1]), ms[2])
        num = jnp.zeros((rows, LANES), F32)
        den = jnp.zeros((rows, LANES), F32)
        for pi in range(len(DILATIONS)):
            e = jnp.exp2(ms[pi] - mx)
            num = num + e * os[pi]
            den = den + e * ls[pi]
        fin_ref[tok, :] = num / den
        return carry

    lax.fori_loop(0, SUPER // rows, combine, 0)

    def gate(c, carry):
        rs = pl.ds(pl.multiple_of(c * rows, rows), rows)
        o_ref[rs, :] = (zd_ref[rs, :].astype(F32) * fin_ref[rs, :]).astype(o_ref.dtype)
        return carry

    lax.fori_loop(0, SUPER // rows, gate, 0)


def _attention4(h1, col0, batch, seq):
    t = batch * seq
    ns = seq // SUPER
    hw = D_HEAD_DIM
    cb0 = col0 // hw
    per = WIDTH // hw
    npat = len(DILATIONS)

    def cur(part):
        return lambda b, g, s: (b * ns + s, cb0 + part * per + g)

    blk = (SUPER, hw)
    return pl.pallas_call(
        _attn4_kernel,
        grid=(batch, per, ns),
        in_specs=[pl.BlockSpec(blk, cur(0)), pl.BlockSpec(blk, cur(1)), pl.BlockSpec(blk, cur(2)),
                  pl.BlockSpec(blk, cur(3))],
        out_specs=pl.BlockSpec(blk, lambda b, g, s: (b * ns + s, g)),
        out_shape=jax.ShapeDtypeStruct((t, WIDTH), BF16),
        scratch_shapes=[pltpu.VMEM((SUPER, LANES), F32),
                        pltpu.VMEM((2 * SUPER, LANES), F32),
                        pltpu.VMEM((2 * SUPER, LANES), F32),
                        pltpu.VMEM((SUPER, LANES), F32),
                        pltpu.VMEM((2 * SUPER, LANES), F32),
                        pltpu.VMEM((2 * SUPER, LANES), F32),
                        pltpu.VMEM((npat, SUPER, LANES), F32),
                        pltpu.VMEM((npat, SUPER, LANES), F32),
                        pltpu.VMEM((npat, SUPER, LANES), F32),
                        pltpu.VMEM((SUPER, LANES), F32),
                        pltpu.VMEM((2, N_BACK, 2 * N_BACK), F32)],
        compiler_params=_cparams("arbitrary", "arbitrary", "arbitrary"),
    )(h1, h1, h1, h1)


def kernel(x, even_norm_g, even_w_in, gmlp_ln_g, gmlp_ln_b, gmlp_ws, gmlp_bs, ssd_conv_w, ssd_conv_b,
           ssd_dt_bias, ssd_a_log, ssd_d, ssd_norm_g, even_w_out, odd_norm_g, odd_w_in, sconv_w,
           odd_w_out, final_norm_g):
    batch, seq, d = x.shape
    assert d == D_MODEL and seq % SUPER == 0
    assert even_norm_g.shape[0] == 1 and odd_norm_g.shape[0] == 1
    t = batch * seq
    x0 = x.reshape(t, d)
    w_in0_t = jnp.swapaxes(even_w_in, 1, 2).reshape(-1, d)
    w_out0 = even_w_out.reshape(-1, d)
    w_in1 = odd_w_in.reshape(d, -1)
    w_out1 = odd_w_out.reshape(-1, d)

    n_main = 4 * WIDTH + B_XBC
    w_dt = jnp.pad(w_in0_t[n_main:, :].T, ((0, 0), (0, LANES - B_HEADS)))
    xn0, dt_raw = _rmsnorm_dt(x0, even_norm_g[0], w_dt)
    h0 = _matmul(xn0, w_in0_t, n_main, [(2 * WIDTH, 4 * WIDTH)], w_is_nk=True)
    x1, xn1 = _mix0(h0, dt_raw, x0, _cast_bf16(w_out0), odd_norm_g[0], batch, seq,
                    gmlp_ln_g[0], gmlp_ln_b[0], gmlp_ws[0], gmlp_bs[0],
                    ssd_conv_w[0], ssd_conv_b[0], ssd_dt_bias[0], ssd_a_log[0], ssd_d[0], ssd_norm_g[0])

    h1 = _matmul(xn1, w_in1, w_in1.shape[1], [(3 * WIDTH, 4 * WIDTH), (7 * WIDTH, 8 * WIDTH)])
    yc = _shortconv(h1, sconv_w[0], batch, seq)
    yd = _attention4(h1, 4 * WIDTH, batch, seq)
    return _outproj_norm(yc, yd, _cast_bf16(w_out1), x1, final_norm_g).reshape(batch, seq, d)
```

```python
import functools

import jax
import jax.numpy as jnp
from jax import lax
from jax.experimental import pallas as pl
from jax.experimental.pallas import tpu as pltpu

F32 = jnp.float32
BF16 = jnp.bfloat16

EPS = 1e-5
D_MODEL = 2048
WIDTH = 2048
A_GROUPS = 8
CHUNK = 128
MIX0_CHUNKS = 1
B_HEAD_DIM = 64
B_HEADS = WIDTH // B_HEAD_DIM
B_GROUPS = 8
B_STATE = 128
B_CONV = 4
B_XBC = WIDTH + 2 * B_GROUPS * B_STATE
HEADS_PER_GROUP = B_HEADS // B_GROUPS
GROUP_W = WIDTH // B_GROUPS
C_CONV = 3
D_HEAD_DIM = 128
D_HEADS = WIDTH // D_HEAD_DIM
N_BACK = 128
DILATIONS = (1, 4, 16)
SUPER = N_BACK * DILATIONS[-1]
LANES = 128
SUBLANES = 8
VMEM_LIMIT = 56 * 1024 * 1024


def _cparams(*sem):
    return pltpu.CompilerParams(dimension_semantics=sem, vmem_limit_bytes=VMEM_LIMIT)


def _silu(z):
    hz = 0.5 * z
    return hz * jnp.tanh(hz) + hz


def _rmsnorm_dt_kernel(x_ref, g_ref, w_ref, o_ref, dt_ref):
    x = x_ref[...]
    ms = jnp.mean(x * x, axis=-1, keepdims=True)
    xn = (x * lax.rsqrt(ms + EPS) * g_ref[...]).astype(o_ref.dtype)
    o_ref[...] = xn
    dt_ref[...] = jnp.dot(xn, w_ref[...].astype(BF16), preferred_element_type=F32)


def _rmsnorm_dt(x2d, g, w_dt, tm=512):
    t, d = x2d.shape
    n = w_dt.shape[1]
    return pl.pallas_call(
        _rmsnorm_dt_kernel,
        grid=(t // tm,),
        in_specs=[pl.BlockSpec((tm, d), lambda i: (i, 0)),
                  pl.BlockSpec((1, d), lambda i: (0, 0)),
                  pl.BlockSpec((d, n), lambda i: (0, 0))],
        out_specs=[pl.BlockSpec((tm, d), lambda i: (i, 0)), pl.BlockSpec((tm, n), lambda i: (i, 0))],
        out_shape=[jax.ShapeDtypeStruct((t, d), BF16), jax.ShapeDtypeStruct((t, n), F32)],
        compiler_params=_cparams("arbitrary"),
    )(x2d, g.reshape(1, d), w_dt)


def _outproj_norm_kernel(a1_ref, a2_ref, w_ref, r_ref, g_ref, o_ref):
    k1 = a1_ref.shape[1]
    x = (jnp.dot(a1_ref[...], w_ref[0:k1, :], preferred_element_type=F32)
         + jnp.dot(a2_ref[...], w_ref[k1:, :], preferred_element_type=F32) + r_ref[...])
    ms = jnp.mean(x * x, axis=-1, keepdims=True)
    o_ref[...] = x * lax.rsqrt(ms + EPS) * g_ref[...]


def _outproj_norm(a1, a2, w_bf16, res, g, tm=512):
    m, k1 = a1.shape
    k2 = a2.shape[1]
    n = w_bf16.shape[1]
    return pl.pallas_call(
        _outproj_norm_kernel,
        grid=(m // tm,),
        in_specs=[pl.BlockSpec((tm, k1), lambda i: (i, 0)),
                  pl.BlockSpec((tm, k2), lambda i: (i, 0)),
                  pl.BlockSpec(w_bf16.shape, lambda i: (0, 0), pipeline_mode=pl.Buffered(1)),
                  pl.BlockSpec((tm, n), lambda i: (i, 0)),
                  pl.BlockSpec((1, n), lambda i: (0, 0))],
        out_specs=pl.BlockSpec((tm, n), lambda i: (i, 0)),
        out_shape=jax.ShapeDtypeStruct((m, n), F32),
        compiler_params=_cparams("arbitrary"),
    )(a1, a2, w_bf16, res, g.reshape(1, n))


def _matmul_kernel(a_ref, w_ref, o_ref, wb_ref, *, cast_rows, w_is_nk):
    @pl.when(pl.program_id(1) == 0)
    def _():
        def body(k, carry):
            r = pl.multiple_of(k * cast_rows, cast_rows)
            wb_ref[pl.ds(r, cast_rows), :] = w_ref[pl.ds(r, cast_rows), :].astype(BF16)
            return carry
        lax.fori_loop(0, w_ref.shape[0] // cast_rows, body, 0)

    dims = (((1,), (1,)), ((), ())) if w_is_nk else (((1,), (0,)), ((), ()))
    o_ref[...] = lax.dot_general(a_ref[...], wb_ref[...], dims, preferred_element_type=F32).astype(o_ref.dtype)


def _matmul(a, w, n_cols, w_is_nk=False, tm=2048, tn=1024):
    m, k = a.shape
    if w_is_nk:
        w_block = pl.BlockSpec((tn, k), lambda j, i: (j, 0))
    else:
        w_block = pl.BlockSpec((k, tn), lambda j, i: (0, j))
    return pl.pallas_call(
        functools.partial(_matmul_kernel, cast_rows=256, w_is_nk=w_is_nk),
        grid=(n_cols // tn, m // tm),
        in_specs=[pl.BlockSpec((tm, k), lambda j, i: (i, 0)), w_block],
        out_specs=pl.BlockSpec((tm, tn), lambda j, i: (i, j)),
        out_shape=jax.ShapeDtypeStruct((m, n_cols), BF16),
        scratch_shapes=[pltpu.VMEM(w_block.block_shape, BF16)],
        compiler_params=_cparams("arbitrary", "arbitrary"),
    )(a, w)


def _softplus(x):
    return jnp.maximum(x, 0.0) + jnp.log1p(jnp.exp(-jnp.abs(x)))


def _split3(x):
    hi = x.astype(BF16)
    r1 = x - hi.astype(F32)
    mid = r1.astype(BF16)
    lo = (r1 - mid.astype(F32)).astype(BF16)
    return hi, mid, lo


def _mix0_kernel(u_ref, v_ref, za_ref, zb_ref, xbc_ref, dt_ref, xres_ref, wout_ref, g1_ref,
                 lng_ref, lnb_ref, ws_ref, bst_ref, cw_ref, cbias_ref, dtb_ref, alog_ref, dexp_ref, ng_ref,
                 x1_ref, xn1_ref,
                 o_ref, yprev_ref, wsb_ref, xbuf_ref, xs_ref, bm_ref, cm_ref, y_ref, st_ref):
    q = CHUNK
    row = lax.broadcasted_iota(jnp.int32, (q, q), 0)
    col = lax.broadcasted_iota(jnp.int32, (q, q), 1)
    causal = col <= row

    @pl.when(pl.program_id(1) == 0)
    def _():
        o_ref[...] = jnp.zeros_like(o_ref)
        xbuf_ref[0:SUBLANES, :] = jnp.zeros((SUBLANES, B_XBC), F32)
        st_ref[...] = jnp.zeros_like(st_ref)
        for g in range(A_GROUPS):
            wsb_ref[g] = jnp.where(causal, ws_ref[g], 0.0).astype(BF16)

    yprev_ref[...] = o_ref[...]
    ncol = 512
    ss = jnp.zeros((o_ref.shape[0], 1), F32)
    for nb in range(D_MODEL // ncol):
        cs = slice(nb * ncol, (nb + 1) * ncol)
        xc = jnp.dot(yprev_ref[...], wout_ref[:, cs], preferred_element_type=F32) + xres_ref[:, cs]
        x1_ref[:, cs] = xc
        ss = ss + jnp.sum(xc * xc, axis=-1, keepdims=True)
    rstd1 = lax.rsqrt(ss * (1.0 / D_MODEL) + EPS)
    for nb in range(D_MODEL // ncol):
        cs = slice(nb * ncol, (nb + 1) * ncol)
        xn1_ref[:, cs] = (x1_ref[:, cs] * rstd1 * g1_ref[:, cs]).astype(xn1_ref.dtype)

    tril = jnp.where(causal, 1.0, 0.0).astype(BF16)
    for ci in range(o_ref.shape[0] // q):
        rs = slice(ci * q, (ci + 1) * q)
        _mix0_chunk(rs, causal, tril, u_ref, v_ref, za_ref, zb_ref, xbc_ref, dt_ref,
                    lng_ref, lnb_ref, bst_ref, cw_ref, cbias_ref, dtb_ref, alog_ref, dexp_ref, ng_ref,
                    o_ref, wsb_ref, xbuf_ref, xs_ref, bm_ref, cm_ref, y_ref, st_ref)
    rows = o_ref.shape[0]
    xbuf_ref[0:SUBLANES, :] = xbuf_ref[rows:rows + SUBLANES, :]


def _mix0_chunk(rs, causal, tril, u_ref, v_ref, za_ref, zb_ref, xbc_ref, dt_ref,
                lng_ref, lnb_ref, bst_ref, cw_ref, cbias_ref, dtb_ref, alog_ref, dexp_ref, ng_ref,
                o_ref, wsb_ref, xbuf_ref, xs_ref, bm_ref, cm_ref, y_ref, st_ref):
    q = CHUNK
    v = v_ref[rs, :].astype(F32)
    mu = jnp.mean(v, axis=-1, keepdims=True)
    xc = v - mu
    var = jnp.mean(xc * xc, axis=-1, keepdims=True)
    rstd = lax.rsqrt(var + EPS)
    for g in range(A_GROUPS):
        sl = slice(g * GROUP_W, (g + 1) * GROUP_W)
        vg = v_ref[rs, sl].astype(F32)
        vn = ((vg - mu) * rstd * lng_ref[:, sl] + lnb_ref[:, sl]).astype(BF16)
        mixed = jnp.dot(wsb_ref[g], vn, preferred_element_type=F32) + bst_ref[:, g:g + 1]
        z = za_ref[rs, sl].astype(F32)
        u = u_ref[rs, sl].astype(F32)
        o_ref[rs, sl] = (_silu(z) * (u * mixed)).astype(o_ref.dtype)

    cwid = 256
    base = SUBLANES + rs.start
    for j in range(B_XBC // cwid):
        sl = slice(j * cwid, (j + 1) * cwid)
        xbuf_ref[base:base + q, sl] = xbc_ref[rs, sl].astype(F32)
        acc = cbias_ref[:, sl]
        for k in range(B_CONV):
            lo = base - (B_CONV - 1) + k
            acc = acc + xbuf_ref[lo:lo + q, sl] * cw_ref[k:k + 1, sl]
        act = _silu(acc)
        lo = j * cwid
        if lo < WIDTH:
            xs_ref[:, lo:lo + cwid] = act
        elif lo < WIDTH + B_GROUPS * B_STATE:
            bm_ref[:, lo - WIDTH:lo - WIDTH + cwid] = act
        else:
            off = lo - WIDTH - B_GROUPS * B_STATE
            cm_ref[:, off:off + cwid] = act

    dt = _softplus(dt_ref[rs, :] + dtb_ref[...])
    adt = dt * (-jnp.exp(alog_ref[...]))
    a_cs = sum(jnp.dot(tril, part, preferred_element_type=F32) for part in _split3(adt))
    a_cs_t = a_cs.T
    dt_t = dt.T

    for g in range(B_GROUPS):
        gs = slice(g * B_STATE, (g + 1) * B_STATE)
        bg = bm_ref[:, gs]
        cgb = cm_ref[:, gs].astype(BF16)
        bg_t = bg.T
        cb = lax.dot_general(cgb, bg.astype(BF16), (((1,), (1,)), ((), ())),
                             preferred_element_type=F32)
        for r in range(HEADS_PER_GROUP):
            h = g * HEADS_PER_GROUP + r
            hs = slice(h * B_HEAD_DIM, (h + 1) * B_HEAD_DIM)
            colv = a_cs[:, h:h + 1]
            rowv = a_cs_t[h:h + 1, :]
            dtr = dt_t[h:h + 1, :]
            decay = jnp.exp(jnp.where(causal, colv - rowv, -jnp.inf))
            mh = (cb * decay * dtr).astype(BF16)
            xh = xs_ref[:, hs]
            xhb = xh.astype(BF16)
            state = st_ref[h]
            y = jnp.dot(mh, xhb, preferred_element_type=F32)
            y = y + jnp.exp(colv) * jnp.dot(cgb, state.astype(BF16), preferred_element_type=F32)
            y_ref[:, hs] = y + dexp_ref[:, hs] * xh
            last = a_cs_t[h:h + 1, q - 1:q]
            wrow = jnp.exp(last - rowv) * dtr
            new = jnp.dot((bg_t * wrow).astype(BF16), xhb, preferred_element_type=F32)
            st_ref[h] = state * jnp.exp(last) + new

    for g in range(B_GROUPS):
        sl = slice(g * GROUP_W, (g + 1) * GROUP_W)
        y = y_ref[:, sl] * _silu(zb_ref[rs, sl].astype(F32))
        ms = jnp.mean(y * y, axis=-1, keepdims=True)
        o_ref[rs, WIDTH + g * GROUP_W:WIDTH + (g + 1) * GROUP_W] = (
            y * lax.rsqrt(ms + EPS) * ng_ref[:, sl]).astype(o_ref.dtype)


def _mix0(h0, dt_raw, x0, w_out, next_norm_g, batch, seq,
          lng, lnb, ws, bs, conv_w, conv_b, dt_bias, a_log, d_skip, norm_g):
    t = batch * seq
    rows = MIX0_CHUNKS * CHUNK
    nc = seq // rows
    pad = LANES - B_HEADS
    row = lambda a: a.reshape(1, -1)
    params = [row(lng), row(lnb), ws, bs.T, conv_w, row(conv_b),
              row(jnp.pad(dt_bias, (0, pad))), row(jnp.pad(a_log, (0, pad))),
              row(jnp.repeat(d_skip, B_HEAD_DIM)), row(norm_g)]
    tok = lambda cb: (lambda b, c: (b * nc + jnp.minimum(c, nc - 1), cb))
    lag = lambda b, c: (b * nc + jnp.maximum(c - 1, 0), 0)
    full = lambda a: pl.BlockSpec(a.shape, lambda b, c: (0,) * a.ndim)
    in_specs = [pl.BlockSpec((rows, WIDTH), tok(0)),
                pl.BlockSpec((rows, WIDTH), tok(1)),
                pl.BlockSpec((rows, WIDTH), tok(2)),
                pl.BlockSpec((rows, WIDTH), tok(3)),
                pl.BlockSpec((rows, B_XBC), tok(2)),
                pl.BlockSpec((rows, LANES), tok(0)),
                pl.BlockSpec((rows, D_MODEL), lag),
                pl.BlockSpec(w_out.shape, lambda b, c: (0, 0), pipeline_mode=pl.Buffered(1)),
                full(row(next_norm_g))]
    in_specs += [full(p) for p in params]
    return pl.pallas_call(
        _mix0_kernel,
        grid=(batch, nc + 1),
        in_specs=in_specs,
        out_specs=[pl.BlockSpec((rows, D_MODEL), lag), pl.BlockSpec((rows, D_MODEL), lag)],
        out_shape=[jax.ShapeDtypeStruct((t, D_MODEL), F32), jax.ShapeDtypeStruct((t, D_MODEL), BF16)],
        scratch_shapes=[pltpu.VMEM((rows, 2 * WIDTH), BF16),
                        pltpu.VMEM((rows, 2 * WIDTH), BF16),
                        pltpu.VMEM((A_GROUPS, CHUNK, CHUNK), BF16),
                        pltpu.VMEM((SUBLANES + rows, B_XBC), F32),
                        pltpu.VMEM((CHUNK, WIDTH), F32),
                        pltpu.VMEM((CHUNK, B_GROUPS * B_STATE), F32),
                        pltpu.VMEM((CHUNK, B_GROUPS * B_STATE), F32),
                        pltpu.VMEM((CHUNK, WIDTH), F32),
                        pltpu.VMEM((B_HEADS, B_STATE, B_HEAD_DIM), F32)],
        compiler_params=_cparams("arbitrary", "arbitrary"),
    )(h0, h0, h0, h0, h0, dt_raw, x0, w_out, row(next_norm_g), *params)


def _cast_kernel(w_ref, o_ref):
    o_ref[...] = w_ref[...].astype(o_ref.dtype)


def _cast_bf16(w, tr=512):
    r, c = w.shape
    return pl.pallas_call(
        _cast_kernel,
        grid=(r // tr,),
        in_specs=[pl.BlockSpec((tr, c), lambda i: (i, 0))],
        out_specs=pl.BlockSpec((tr, c), lambda i: (i, 0)),
        out_shape=jax.ShapeDtypeStruct((r, c), BF16),
        compiler_params=_cparams("arbitrary"),
    )(w)


def _shortconv_kernel(bg_ref, cg_ref, hx_ref, zc_ref, cw_ref, o_ref, xbuf_ref):
    tile = o_ref.shape[0]

    @pl.when(pl.program_id(1) == 0)
    def _():
        xbuf_ref[0:SUBLANES, :] = jnp.zeros((SUBLANES, WIDTH), F32)

    cwid, rows = 256, 128
    for j in range(WIDTH // cwid):
        sl = slice(j * cwid, (j + 1) * cwid)
        for i in range(tile // rows):
            r0 = i * rows
            xbuf_ref[SUBLANES + r0:SUBLANES + r0 + rows, sl] = (
                cg_ref[r0:r0 + rows, sl].astype(F32) * hx_ref[r0:r0 + rows, sl].astype(F32))
        for i in range(tile // rows):
            r0 = i * rows
            acc = None
            for k in range(C_CONV):
                lo = SUBLANES - (C_CONV - 1) + k + r0
                term = xbuf_ref[lo:lo + rows, sl] * cw_ref[k:k + 1, sl]
                acc = term if acc is None else acc + term
            gate = _silu(zc_ref[r0:r0 + rows, sl].astype(F32)) * bg_ref[r0:r0 + rows, sl].astype(F32)
            o_ref[r0:r0 + rows, sl] = (gate * acc).astype(o_ref.dtype)
        xbuf_ref[0:SUBLANES, sl] = xbuf_ref[tile:tile + SUBLANES, sl]


def _shortconv(h1, conv_w, batch, seq, tile=512):
    t = batch * seq
    nt = seq // tile
    tok = lambda cb: (lambda b, i: (b * nt + i, cb))
    return pl.pallas_call(
        _shortconv_kernel,
        grid=(batch, nt),
        in_specs=[pl.BlockSpec((tile, WIDTH), tok(0)), pl.BlockSpec((tile, WIDTH), tok(1)),
                  pl.BlockSpec((tile, WIDTH), tok(2)), pl.BlockSpec((tile, WIDTH), tok(3)),
                  pl.BlockSpec(conv_w.shape, lambda b, i: (0, 0))],
        out_specs=pl.BlockSpec((tile, WIDTH), tok(0)),
        out_shape=jax.ShapeDtypeStruct((t, WIDTH), BF16),
        scratch_shapes=[pltpu.VMEM((SUBLANES + tile, WIDTH), F32)],
        compiler_params=_cparams("arbitrary", "arbitrary"),
    )(h1, h1, h1, h1, conv_w)


R4 = DILATIONS[1]
ROWS4 = SUPER // R4


def _attn4_kernel(q_ref, kc_ref, vc_ref, zd_ref, o_ref,
                  qn_ref, kn_ref, vn_ref, q4_ref, k4_ref, v4_ref, os_ref, ls_ref, ms_ref, fin_ref, bias_ref):
    first_super = pl.program_id(2) == 0
    cur0 = pl.multiple_of((pl.program_id(2) % 2) * SUPER, SUPER)
    prev0 = pl.multiple_of(SUPER - cur0, SUPER)
    nq = N_BACK
    blocks = SUPER // nq
    a = lax.broadcasted_iota(jnp.int32, (nq, 2 * nq), 0)
    j = lax.broadcasted_iota(jnp.int32, (nq, 2 * nq), 1)
    in_band = jnp.logical_and(j >= a, j <= a + nq)
    bias_ref[0] = jnp.where(in_band, 0.0, -jnp.inf)
    bias_ref[1] = jnp.where(jnp.logical_and(in_band, j >= nq), 0.0, -jnp.inf)
    qscale = D_HEAD_DIM ** -0.5 * 1.4426950408889634
    ones_cols = jnp.ones((2 * nq, LANES), BF16)

    @pl.when(first_super)
    def _():
        for ref in (kn_ref, vn_ref, k4_ref, v4_ref):
            ref[pl.ds(prev0, SUPER), :] = jnp.zeros((SUPER, LANES), F32)

    qn_ref[...] = q_ref[...].astype(F32) * qscale
    kn_ref[pl.ds(cur0, SUPER), :] = kc_ref[...].astype(F32)
    vn_ref[pl.ds(cur0, SUPER), :] = vc_ref[...].astype(F32)
    for r in range(R4):
        q4_ref[r * ROWS4:(r + 1) * ROWS4] = qn_ref[pl.ds(r, ROWS4, stride=R4), :]
        dst = pl.ds(pl.multiple_of(cur0 + r * ROWS4, ROWS4), ROWS4)
        k4_ref[dst, :] = kn_ref[pl.ds(cur0 + r, ROWS4, stride=R4), :]
        v4_ref[dst, :] = vn_ref[pl.ds(cur0 + r, ROWS4, stride=R4), :]

    def softmax_block(qf, kf, vf, no_prev):
        s = lax.dot_general(qf.astype(BF16), kf.astype(BF16), (((1,), (1,)), ((), ())),
                            preferred_element_type=F32) + bias_ref[no_prev.astype(jnp.int32)]
        m = jnp.max(s, axis=-1, keepdims=True)
        p = jnp.exp2(s - m).astype(BF16)
        o2 = jnp.dot(p, jnp.concatenate([vf.astype(BF16), ones_cols], axis=1), preferred_element_type=F32)
        return o2[:, :LANES], o2[:, LANES:], jnp.broadcast_to(m, (nq, LANES))

    def block(i, carry):
        q1 = pl.ds(pl.multiple_of(i * nq, nq), nq)
        kc1 = pl.ds(pl.multiple_of(cur0 + i * nq, nq), nq)
        kp1 = pl.ds(pl.multiple_of(jnp.where(i > 0, cur0 + (i - 1) * nq, prev0 + SUPER - nq), nq), nq)
        in1 = (qn_ref[q1, :],
               jnp.concatenate([kn_ref[kp1, :], kn_ref[kc1, :]], axis=0),
               jnp.concatenate([vn_ref[kp1, :], vn_ref[kc1, :]], axis=0),
               jnp.logical_and(first_super, i == 0))
        res, sub = i // (ROWS4 // nq), i % (ROWS4 // nq)
        q4 = pl.ds(pl.multiple_of(res * ROWS4 + sub * nq, nq), nq)
        kc4 = pl.ds(pl.multiple_of(cur0 + res * ROWS4 + sub * nq, nq), nq)
        kp4 = pl.ds(pl.multiple_of(jnp.where(sub > 0, cur0 + res * ROWS4 + (sub - 1) * nq,
                                             prev0 + res * ROWS4 + ROWS4 - nq), nq), nq)
        in4 = (q4_ref[q4, :],
               jnp.concatenate([k4_ref[kp4, :], k4_ref[kc4, :]], axis=0),
               jnp.concatenate([v4_ref[kp4, :], v4_ref[kc4, :]], axis=0),
               jnp.logical_and(first_super, sub == 0))
        lo, hi = i % R4, i // R4
        q16 = pl.ds(lo * ROWS4 + hi, nq, stride=R4)
        kp16 = pl.ds(prev0 + lo * ROWS4 + hi, nq, stride=R4)
        kc16 = pl.ds(cur0 + lo * ROWS4 + hi, nq, stride=R4)
        in16 = (q4_ref[q16, :],
                jnp.concatenate([k4_ref[kp16, :], k4_ref[kc16, :]], axis=0),
                jnp.concatenate([v4_ref[kp16, :], v4_ref[kc16, :]], axis=0),
                first_super)
        outs = [softmax_block(*args) for args in (in1, in4, in16)]
        for pi, qsl in enumerate((q1, q4, q16)):
            os_ref[pi, qsl, :] = outs[pi][0]
            ls_ref[pi, qsl, :] = outs[pi][1]
            ms_ref[pi, qsl, :] = outs[pi][2]
        return carry

    lax.fori_loop(0, blocks, block, 0, unroll=8)

    rows = 256

    def combine(c, carry):
        home = pl.ds(pl.multiple_of(c * rows, rows), rows)
        r, part = c // (ROWS4 // rows), c % (ROWS4 // rows)
        tok = pl.ds(r + part * (rows * R4), rows, stride=R4)
        ms = [ms_ref[0, tok, :], ms_ref[1, home, :], ms_ref[2, home, :]]
        os = [os_ref[0, tok, :], os_ref[1, home, :], os_ref[2, home, :]]
        ls = [ls_ref[0, tok, :], ls_ref[1, home, :], ls_ref[2, home, :]]
        mx = jnp.maximum(jnp.maximum(ms[0], ms[1]), ms[2])
        num = jnp.zeros((rows, LANES), F32)
        den = jnp.zeros((rows, LANES), F32)
        for pi in range(len(DILATIONS)):
            e = jnp.exp2(ms[pi] - mx)
            num = num + e * os[pi]
            den = den + e * ls[pi]
        fin_ref[tok, :] = num / den
        return carry

    lax.fori_loop(0, SUPER // rows, combine, 0)

    def gate(c, carry):
        rs = pl.ds(pl.multiple_of(c * rows, rows), rows)
        o_ref[rs, :] = (_silu(zd_ref[rs, :].astype(F32)) * fin_ref[rs, :]).astype(o_ref.dtype)
        return carry

    lax.fori_loop(0, SUPER // rows, gate, 0)


def _attention4(h1, col0, batch, seq):
    t = batch * seq
    ns = seq // SUPER
    hw = D_HEAD_DIM
    cb0 = col0 // hw
    per = WIDTH // hw
    npat = len(DILATIONS)

    def cur(part):
        return lambda b, g, s: (b * ns + s, cb0 + part * per + g)

    blk = (SUPER, hw)
    return pl.pallas_call(
        _attn4_kernel,
        grid=(batch, per, ns),
        in_specs=[pl.BlockSpec(blk, cur(0)), pl.BlockSpec(blk, cur(1)), pl.BlockSpec(blk, cur(2)),
                  pl.BlockSpec(blk, cur(3))],
        out_specs=pl.BlockSpec(blk, lambda b, g, s: (b * ns + s, g)),
        out_shape=jax.ShapeDtypeStruct((t, WIDTH), BF16),
        scratch_shapes=[pltpu.VMEM((SUPER, LANES), F32),
                        pltpu.VMEM((2 * SUPER, LANES), F32),
                        pltpu.VMEM((2 * SUPER, LANES), F32),
                        pltpu.VMEM((SUPER, LANES), F32),
                        pltpu.VMEM((2 * SUPER, LANES), F32),
                        pltpu.VMEM((2 * SUPER, LANES), F32),
                        pltpu.VMEM((npat, SUPER, LANES), F32),
                        pltpu.VMEM((npat, SUPER, LANES), F32),
                        pltpu.VMEM((npat, SUPER, LANES), F32),
                        pltpu.VMEM((SUPER, LANES), F32),
                        pltpu.VMEM((2, N_BACK, 2 * N_BACK), F32)],
        compiler_params=_cparams("arbitrary", "arbitrary", "arbitrary"),
    )(h1, h1, h1, h1)


def kernel(x, even_norm_g, even_w_in, gmlp_ln_g, gmlp_ln_b, gmlp_ws, gmlp_bs, ssd_conv_w, ssd_conv_b,
           ssd_dt_bias, ssd_a_log, ssd_d, ssd_norm_g, even_w_out, odd_norm_g, odd_w_in, sconv_w,
           odd_w_out, final_norm_g):
    batch, seq, d = x.shape
    assert d == D_MODEL and seq % SUPER == 0
    assert even_norm_g.shape[0] == 1 and odd_norm_g.shape[0] == 1
    t = batch * seq
    x0 = x.reshape(t, d)
    w_in0_t = jnp.swapaxes(even_w_in, 1, 2).reshape(-1, d)
    w_out0 = even_w_out.reshape(-1, d)
    w_in1 = odd_w_in.reshape(d, -1)
    w_out1 = odd_w_out.reshape(-1, d)

    n_main = 4 * WIDTH + B_XBC
    w_dt = jnp.pad(w_in0_t[n_main:, :].T, ((0, 0), (0, LANES - B_HEADS)))
    xn0, dt_raw = _rmsnorm_dt(x0, even_norm_g[0], w_dt)
    h0 = _matmul(xn0, w_in0_t, n_main, w_is_nk=True)
    x1, xn1 = _mix0(h0, dt_raw, x0, _cast_bf16(w_out0), odd_norm_g[0], batch, seq,
                    gmlp_ln_g[0], gmlp_ln_b[0], gmlp_ws[0], gmlp_bs[0],
                    ssd_conv_w[0], ssd_conv_b[0], ssd_dt_bias[0], ssd_a_log[0], ssd_d[0], ssd_norm_g[0])

    h1 = _matmul(xn1, w_in1, w_in1.shape[1])
    yc = _shortconv(h1, sconv_w[0], batch, seq)
    yd = _attention4(h1, 4 * WIDTH, batch, seq)
    return _outproj_norm(yc, yd, _cast_bf16(w_out1), x1, final_norm_g).reshape(batch, seq, d)
```

```python
import functools

import jax
import jax.numpy as jnp
from jax import lax
from jax.experimental import pallas as pl
from jax.experimental.pallas import tpu as pltpu

F32 = jnp.float32
BF16 = jnp.bfloat16

EPS = 1e-5
D_MODEL = 2048
WIDTH = 2048
A_GROUPS = 8
CHUNK = 128
MIX0_CHUNKS = 1
B_HEAD_DIM = 64
B_HEADS = WIDTH // B_HEAD_DIM
B_GROUPS = 8
B_STATE = 128
B_CONV = 4
B_XBC = WIDTH + 2 * B_GROUPS * B_STATE
HEADS_PER_GROUP = B_HEADS // B_GROUPS
GROUP_W = WIDTH // B_GROUPS
C_CONV = 3
D_HEAD_DIM = 128
D_HEADS = WIDTH // D_HEAD_DIM
N_BACK = 128
DILATIONS = (1, 4, 16)
SUPER = N_BACK * DILATIONS[-1]
LANES = 128
SUBLANES = 8
VMEM_LIMIT = 56 * 1024 * 1024


def _cparams(*sem):
    return pltpu.CompilerParams(dimension_semantics=sem, vmem_limit_bytes=VMEM_LIMIT)


def _silu(z):
    hz = 0.5 * z
    return hz * jnp.tanh(hz) + hz


def _rmsnorm_dt_kernel(x_ref, g_ref, w_ref, o_ref, dt_ref):
    x = x_ref[...]
    ms = jnp.mean(x * x, axis=-1, keepdims=True)
    xn = (x * lax.rsqrt(ms + EPS) * g_ref[...]).astype(o_ref.dtype)
    o_ref[...] = xn
    dt_ref[...] = jnp.dot(xn, w_ref[...].astype(BF16), preferred_element_type=F32)


def _rmsnorm_dt(x2d, g, w_dt, tm=512):
    t, d = x2d.shape
    n = w_dt.shape[1]
    return pl.pallas_call(
        _rmsnorm_dt_kernel,
        grid=(t // tm,),
        in_specs=[pl.BlockSpec((tm, d), lambda i: (i, 0)),
                  pl.BlockSpec((1, d), lambda i: (0, 0)),
                  pl.BlockSpec((d, n), lambda i: (0, 0))],
        out_specs=[pl.BlockSpec((tm, d), lambda i: (i, 0)), pl.BlockSpec((tm, n), lambda i: (i, 0))],
        out_shape=[jax.ShapeDtypeStruct((t, d), BF16), jax.ShapeDtypeStruct((t, n), F32)],
        compiler_params=_cparams("arbitrary"),
    )(x2d, g.reshape(1, d), w_dt)


def _outproj_norm_kernel(a1_ref, a2_ref, w_ref, r_ref, g_ref, o_ref):
    k1 = a1_ref.shape[1]
    x = (jnp.dot(a1_ref[...], w_ref[0:k1, :], preferred_element_type=F32)
         + jnp.dot(a2_ref[...], w_ref[k1:, :], preferred_element_type=F32) + r_ref[...])
    ms = jnp.mean(x * x, axis=-1, keepdims=True)
    o_ref[...] = x * lax.rsqrt(ms + EPS) * g_ref[...]


def _outproj_norm(a1, a2, w_bf16, res, g, tm=512):
    m, k1 = a1.shape
    k2 = a2.shape[1]
    n = w_bf16.shape[1]
    return pl.pallas_call(
        _outproj_norm_kernel,
        grid=(m // tm,),
        in_specs=[pl.BlockSpec((tm, k1), lambda i: (i, 0)),
                  pl.BlockSpec((tm, k2), lambda i: (i, 0)),
                  pl.BlockSpec(w_bf16.shape, lambda i: (0, 0), pipeline_mode=pl.Buffered(1)),
                  pl.BlockSpec((tm, n), lambda i: (i, 0)),
                  pl.BlockSpec((1, n), lambda i: (0, 0))],
        out_specs=pl.BlockSpec((tm, n), lambda i: (i, 0)),
        out_shape=jax.ShapeDtypeStruct((m, n), F32),
        compiler_params=_cparams("arbitrary"),
    )(a1, a2, w_bf16, res, g.reshape(1, n))


def _matmul_kernel(a_ref, w_ref, o_ref, wb_ref, *, cast_rows, w_is_nk):
    @pl.when(pl.program_id(1) == 0)
    def _():
        def body(k, carry):
            r = pl.multiple_of(k * cast_rows, cast_rows)
            wb_ref[pl.ds(r, cast_rows), :] = w_ref[pl.ds(r, cast_rows), :].astype(BF16)
            return carry
        lax.fori_loop(0, w_ref.shape[0] // cast_rows, body, 0)

    dims = (((1,), (1,)), ((), ())) if w_is_nk else (((1,), (0,)), ((), ()))
    o_ref[...] = lax.dot_general(a_ref[...], wb_ref[...], dims, preferred_element_type=F32).astype(o_ref.dtype)


def _matmul(a, w, n_cols, w_is_nk=False, tm=2048, tn=1024):
    m, k = a.shape
    if w_is_nk:
        w_block = pl.BlockSpec((tn, k), lambda j, i: (j, 0))
    else:
        w_block = pl.BlockSpec((k, tn), lambda j, i: (0, j))
    return pl.pallas_call(
        functools.partial(_matmul_kernel, cast_rows=256, w_is_nk=w_is_nk),
        grid=(n_cols // tn, m // tm),
        in_specs=[pl.BlockSpec((tm, k), lambda j, i: (i, 0)), w_block],
        out_specs=pl.BlockSpec((tm, tn), lambda j, i: (i, j)),
        out_shape=jax.ShapeDtypeStruct((m, n_cols), BF16),
        scratch_shapes=[pltpu.VMEM(w_block.block_shape, BF16)],
        compiler_params=_cparams("arbitrary", "arbitrary"),
    )(a, w)


def _softplus(x):
    return jnp.maximum(x, 0.0) + jnp.log1p(jnp.exp(-jnp.abs(x)))


def _split3(x):
    hi = x.astype(BF16)
    r1 = x - hi.astype(F32)
    mid = r1.astype(BF16)
    lo = (r1 - mid.astype(F32)).astype(BF16)
    return hi, mid, lo


def _mix0_kernel(u_ref, v_ref, za_ref, zb_ref, xbc_ref, dt_ref, xres_ref, wout_ref, g1_ref,
                 lng_ref, lnb_ref, ws_ref, bst_ref, cw_ref, cbias_ref, dtb_ref, alog_ref, dexp_ref, ng_ref,
                 x1_ref, xn1_ref,
                 o_ref, yprev_ref, wsb_ref, xbuf_ref, xs_ref, bm_ref, cm_ref, y_ref, st_ref):
    q = CHUNK
    row = lax.broadcasted_iota(jnp.int32, (q, q), 0)
    col = lax.broadcasted_iota(jnp.int32, (q, q), 1)
    causal = col <= row

    @pl.when(pl.program_id(1) == 0)
    def _():
        o_ref[...] = jnp.zeros_like(o_ref)
        xbuf_ref[0:SUBLANES, :] = jnp.zeros((SUBLANES, B_XBC), F32)
        st_ref[...] = jnp.zeros_like(st_ref)
        for g in range(A_GROUPS):
            wsb_ref[g] = jnp.where(causal, ws_ref[g], 0.0).astype(BF16)

    yprev_ref[...] = o_ref[...]
    ncol = 512
    ss = jnp.zeros((o_ref.shape[0], 1), F32)
    for nb in range(D_MODEL // ncol):
        cs = slice(nb * ncol, (nb + 1) * ncol)
        xc = jnp.dot(yprev_ref[...], wout_ref[:, cs], preferred_element_type=F32) + xres_ref[:, cs]
        x1_ref[:, cs] = xc
        ss = ss + jnp.sum(xc * xc, axis=-1, keepdims=True)
    rstd1 = lax.rsqrt(ss * (1.0 / D_MODEL) + EPS)
    for nb in range(D_MODEL // ncol):
        cs = slice(nb * ncol, (nb + 1) * ncol)
        xn1_ref[:, cs] = (x1_ref[:, cs] * rstd1 * g1_ref[:, cs]).astype(xn1_ref.dtype)

    tril = jnp.where(causal, 1.0, 0.0).astype(BF16)
    for ci in range(o_ref.shape[0] // q):
        rs = slice(ci * q, (ci + 1) * q)
        _mix0_chunk(rs, causal, tril, u_ref, v_ref, za_ref, zb_ref, xbc_ref, dt_ref,
                    lng_ref, lnb_ref, bst_ref, cw_ref, cbias_ref, dtb_ref, alog_ref, dexp_ref, ng_ref,
                    o_ref, wsb_ref, xbuf_ref, xs_ref, bm_ref, cm_ref, y_ref, st_ref)
    rows = o_ref.shape[0]
    xbuf_ref[0:SUBLANES, :] = xbuf_ref[rows:rows + SUBLANES, :]


def _mix0_chunk(rs, causal, tril, u_ref, v_ref, za_ref, zb_ref, xbc_ref, dt_ref,
                lng_ref, lnb_ref, bst_ref, cw_ref, cbias_ref, dtb_ref, alog_ref, dexp_ref, ng_ref,
                o_ref, wsb_ref, xbuf_ref, xs_ref, bm_ref, cm_ref, y_ref, st_ref):
    q = CHUNK
    v = v_ref[rs, :].astype(F32)
    mu = jnp.mean(v, axis=-1, keepdims=True)
    xc = v - mu
    var = jnp.mean(xc * xc, axis=-1, keepdims=True)
    rstd = lax.rsqrt(var + EPS)
    for g in range(A_GROUPS):
        sl = slice(g * GROUP_W, (g + 1) * GROUP_W)
        vg = v_ref[rs, sl].astype(F32)
        vn = ((vg - mu) * rstd * lng_ref[:, sl] + lnb_ref[:, sl]).astype(BF16)
        mixed = jnp.dot(wsb_ref[g], vn, preferred_element_type=F32) + bst_ref[:, g:g + 1]
        z = za_ref[rs, sl].astype(F32)
        u = u_ref[rs, sl].astype(F32)
        o_ref[rs, sl] = (_silu(z) * (u * mixed)).astype(o_ref.dtype)

    cwid = 256
    base = SUBLANES + rs.start
    for j in range(B_XBC // cwid):
        sl = slice(j * cwid, (j + 1) * cwid)
        xbuf_ref[base:base + q, sl] = xbc_ref[rs, sl].astype(F32)
        acc = cbias_ref[:, sl]
        for k in range(B_CONV):
            lo = base - (B_CONV - 1) + k
            acc = acc + xbuf_ref[lo:lo + q, sl] * cw_ref[k:k + 1, sl]
        act = _silu(acc)
        lo = j * cwid
        if lo < WIDTH:
            xs_ref[:, lo:lo + cwid] = act
        elif lo < WIDTH + B_GROUPS * B_STATE:
            bm_ref[:, lo - WIDTH:lo - WIDTH + cwid] = act
        else:
            off = lo - WIDTH - B_GROUPS * B_STATE
            cm_ref[:, off:off + cwid] = act

    dt = _softplus(dt_ref[rs, :] + dtb_ref[...])
    adt = dt * (-jnp.exp(alog_ref[...]))
    a_cs = sum(jnp.dot(tril, part, preferred_element_type=F32) for part in _split3(adt))
    a_cs_t = a_cs.T
    dt_t = dt.T

    for g in range(B_GROUPS):
        gs = slice(g * B_STATE, (g + 1) * B_STATE)
        bg = bm_ref[:, gs]
        cgb = cm_ref[:, gs].astype(BF16)
        bg_t = bg.T
        cb = lax.dot_general(cgb, bg.astype(BF16), (((1,), (1,)), ((), ())),
                             preferred_element_type=F32)
        for r in range(HEADS_PER_GROUP):
            h = g * HEADS_PER_GROUP + r
            hs = slice(h * B_HEAD_DIM, (h + 1) * B_HEAD_DIM)
            colv = a_cs[:, h:h + 1]
            rowv = a_cs_t[h:h + 1, :]
            dtr = dt_t[h:h + 1, :]
            decay = jnp.exp(jnp.where(causal, colv - rowv, -jnp.inf))
            mh = (cb * decay * dtr).astype(BF16)
            xh = xs_ref[:, hs]
            xhb = xh.astype(BF16)
            state = st_ref[h]
            y = jnp.dot(mh, xhb, preferred_element_type=F32)
            y = y + jnp.exp(colv) * jnp.dot(cgb, state.astype(BF16), preferred_element_type=F32)
            y_ref[:, hs] = y + dexp_ref[:, hs] * xh
            last = a_cs_t[h:h + 1, q - 1:q]
            wrow = jnp.exp(last - rowv) * dtr
            new = jnp.dot((bg_t * wrow).astype(BF16), xhb, preferred_element_type=F32)
            st_ref[h] = state * jnp.exp(last) + new

    for g in range(B_GROUPS):
        sl = slice(g * GROUP_W, (g + 1) * GROUP_W)
        y = y_ref[:, sl] * _silu(zb_ref[rs, sl].astype(F32))
        ms = jnp.mean(y * y, axis=-1, keepdims=True)
        o_ref[rs, WIDTH + g * GROUP_W:WIDTH + (g + 1) * GROUP_W] = (
            y * lax.rsqrt(ms + EPS) * ng_ref[:, sl]).astype(o_ref.dtype)


def _mix0(h0, dt_raw, x0, w_out, next_norm_g, batch, seq,
          lng, lnb, ws, bs, conv_w, conv_b, dt_bias, a_log, d_skip, norm_g):
    t = batch * seq
    rows = MIX0_CHUNKS * CHUNK
    nc = seq // rows
    pad = LANES - B_HEADS
    row = lambda a: a.reshape(1, -1)
    params = [row(lng), row(lnb), ws, bs.T, conv_w, row(conv_b),
              row(jnp.pad(dt_bias, (0, pad))), row(jnp.pad(a_log, (0, pad))),
              row(jnp.repeat(d_skip, B_HEAD_DIM)), row(norm_g)]
    tok = lambda cb: (lambda b, c: (b * nc + jnp.minimum(c, nc - 1), cb))
    lag = lambda b, c: (b * nc + jnp.maximum(c - 1, 0), 0)
    full = lambda a: pl.BlockSpec(a.shape, lambda b, c: (0,) * a.ndim)
    in_specs = [pl.BlockSpec((rows, WIDTH), tok(0)),
                pl.BlockSpec((rows, WIDTH), tok(1)),
                pl.BlockSpec((rows, WIDTH), tok(2)),
                pl.BlockSpec((rows, WIDTH), tok(3)),
                pl.BlockSpec((rows, B_XBC), tok(2)),
                pl.BlockSpec((rows, LANES), tok(0)),
                pl.BlockSpec((rows, D_MODEL), lag),
                pl.BlockSpec(w_out.shape, lambda b, c: (0, 0), pipeline_mode=pl.Buffered(1)),
                full(row(next_norm_g))]
    in_specs += [full(p) for p in params]
    return pl.pallas_call(
        _mix0_kernel,
        grid=(batch, nc + 1),
        in_specs=in_specs,
        out_specs=[pl.BlockSpec((rows, D_MODEL), lag), pl.BlockSpec((rows, D_MODEL), lag)],
        out_shape=[jax.ShapeDtypeStruct((t, D_MODEL), F32), jax.ShapeDtypeStruct((t, D_MODEL), BF16)],
        scratch_shapes=[pltpu.VMEM((rows, 2 * WIDTH), BF16),
                        pltpu.VMEM((rows, 2 * WIDTH), BF16),
                        pltpu.VMEM((A_GROUPS, CHUNK, CHUNK), BF16),
                        pltpu.VMEM((SUBLANES + rows, B_XBC), F32),
                        pltpu.VMEM((CHUNK, WIDTH), F32),
                        pltpu.VMEM((CHUNK, B_GROUPS * B_STATE), F32),
                        pltpu.VMEM((CHUNK, B_GROUPS * B_STATE), F32),
                        pltpu.VMEM((CHUNK, WIDTH), F32),
                        pltpu.VMEM((B_HEADS, B_STATE, B_HEAD_DIM), F32)],
        compiler_params=_cparams("arbitrary", "arbitrary"),
    )(h0, h0, h0, h0, h0, dt_raw, x0, w_out, row(next_norm_g), *params)


def _cast_kernel(w_ref, o_ref):
    o_ref[...] = w_ref[...].astype(o_ref.dtype)


def _cast_bf16(w, tr=512):
    r, c = w.shape
    return pl.pallas_call(
        _cast_kernel,
        grid=(r // tr,),
        in_specs=[pl.BlockSpec((tr, c), lambda i: (i, 0))],
        out_specs=pl.BlockSpec((tr, c), lambda i: (i, 0)),
        out_shape=jax.ShapeDtypeStruct((r, c), BF16),
        compiler_params=_cparams("arbitrary"),
    )(w)


def _shortconv_kernel(bg_ref, cg_ref, hx_ref, zc_ref, cw_ref, o_ref, xbuf_ref):
    tile = o_ref.shape[0]

    @pl.when(pl.program_id(1) == 0)
    def _():
        xbuf_ref[0:SUBLANES, :] = jnp.zeros((SUBLANES, WIDTH), F32)

    cwid, rows = 256, 128
    for j in range(WIDTH // cwid):
        sl = slice(j * cwid, (j + 1) * cwid)
        for i in range(tile // rows):
            r0 = i * rows
            xbuf_ref[SUBLANES + r0:SUBLANES + r0 + rows, sl] = (
                cg_ref[r0:r0 + rows, sl].astype(F32) * hx_ref[r0:r0 + rows, sl].astype(F32))
        for i in range(tile // rows):
            r0 = i * rows
            acc = None
            for k in range(C_CONV):
                lo = SUBLANES - (C_CONV - 1) + k + r0
                term = xbuf_ref[lo:lo + rows, sl] * cw_ref[k:k + 1, sl]
                acc = term if acc is None else acc + term
            gate = _silu(zc_ref[r0:r0 + rows, sl].astype(F32)) * bg_ref[r0:r0 + rows, sl].astype(F32)
            o_ref[r0:r0 + rows, sl] = (gate * acc).astype(o_ref.dtype)
        xbuf_ref[0:SUBLANES, sl] = xbuf_ref[tile:tile + SUBLANES, sl]


def _shortconv(h1, conv_w, batch, seq, tile=512):
    t = batch * seq
    nt = seq // tile
    tok = lambda cb: (lambda b, i: (b * nt + i, cb))
    return pl.pallas_call(
        _shortconv_kernel,
        grid=(batch, nt),
        in_specs=[pl.BlockSpec((tile, WIDTH), tok(0)), pl.BlockSpec((tile, WIDTH), tok(1)),
                  pl.BlockSpec((tile, WIDTH), tok(2)), pl.BlockSpec((tile, WIDTH), tok(3)),
                  pl.BlockSpec(conv_w.shape, lambda b, i: (0, 0))],
        out_specs=pl.BlockSpec((tile, WIDTH), tok(0)),
        out_shape=jax.ShapeDtypeStruct((t, WIDTH), BF16),
        scratch_shapes=[pltpu.VMEM((SUBLANES + tile, WIDTH), F32)],
        compiler_params=_cparams("arbitrary", "arbitrary"),
    )(h1, h1, h1, h1, conv_w)


R4 = DILATIONS[1]
ROWS4 = SUPER // R4


def _attn4_kernel(q_ref, kc_ref, vc_ref, zd_ref, o_ref,
                  qn_ref, kn_ref, vn_ref, q4_ref, k4_ref, v4_ref, os_ref, ls_ref, ms_ref, fin_ref, bias_ref):
    first_super = pl.program_id(2) == 0
    cur0 = pl.multiple_of((pl.program_id(2) % 2) * SUPER, SUPER)
    prev0 = pl.multiple_of(SUPER - cur0, SUPER)
    nq = N_BACK
    blocks = SUPER // nq
    a = lax.broadcasted_iota(jnp.int32, (nq, 2 * nq), 0)
    j = lax.broadcasted_iota(jnp.int32, (nq, 2 * nq), 1)
    in_band = jnp.logical_and(j >= a, j <= a + nq)
    bias_ref[0] = jnp.where(in_band, 0.0, -jnp.inf)
    bias_ref[1] = jnp.where(jnp.logical_and(in_band, j >= nq), 0.0, -jnp.inf)
    qscale = D_HEAD_DIM ** -0.5 * 1.4426950408889634
    ones_cols = jnp.ones((2 * nq, LANES), BF16)

    @pl.when(first_super)
    def _():
        for ref in (kn_ref, vn_ref, k4_ref, v4_ref):
            ref[pl.ds(prev0, SUPER), :] = jnp.zeros((SUPER, LANES), F32)

    qn_ref[...] = q_ref[...].astype(F32) * qscale
    kn_ref[pl.ds(cur0, SUPER), :] = kc_ref[...].astype(F32)
    vn_ref[pl.ds(cur0, SUPER), :] = vc_ref[...].astype(F32)
    for r in range(R4):
        q4_ref[r * ROWS4:(r + 1) * ROWS4] = qn_ref[pl.ds(r, ROWS4, stride=R4), :]
        dst = pl.ds(pl.multiple_of(cur0 + r * ROWS4, ROWS4), ROWS4)
        k4_ref[dst, :] = kn_ref[pl.ds(cur0 + r, ROWS4, stride=R4), :]
        v4_ref[dst, :] = vn_ref[pl.ds(cur0 + r, ROWS4, stride=R4), :]

    def softmax_block(qf, kf, vf, no_prev):
        bias = bias_ref[0] if no_prev is None else bias_ref[no_prev.astype(jnp.int32)]
        s = lax.dot_general(qf.astype(BF16), kf.astype(BF16), (((1,), (1,)), ((), ())),
                            preferred_element_type=F32) + bias
        m = jnp.max(s, axis=-1, keepdims=True)
        p = jnp.exp2(s - m).astype(BF16)
        o2 = jnp.dot(p, jnp.concatenate([vf.astype(BF16), ones_cols], axis=1), preferred_element_type=F32)
        return o2[:, :LANES], o2[:, LANES:], jnp.broadcast_to(m, (nq, LANES))

    def rows_of(base, off):
        return pl.ds(pl.multiple_of(base + off, nq), nq)

    never = None
    for i in range(blocks):
        q1 = pl.ds(i * nq, nq)
        kc1 = rows_of(cur0, i * nq)
        kp1 = rows_of(cur0, (i - 1) * nq) if i > 0 else rows_of(prev0, SUPER - nq)
        in1 = (qn_ref[q1, :],
               jnp.concatenate([kn_ref[kp1, :], kn_ref[kc1, :]], axis=0),
               jnp.concatenate([vn_ref[kp1, :], vn_ref[kc1, :]], axis=0),
               first_super if i == 0 else never)
        res, sub = divmod(i, ROWS4 // nq)
        q4 = pl.ds(res * ROWS4 + sub * nq, nq)
        kc4 = rows_of(cur0, res * ROWS4 + sub * nq)
        kp4 = rows_of(cur0, res * ROWS4 + (sub - 1) * nq) if sub > 0 else rows_of(prev0, res * ROWS4 + ROWS4 - nq)
        in4 = (q4_ref[q4, :],
               jnp.concatenate([k4_ref[kp4, :], k4_ref[kc4, :]], axis=0),
               jnp.concatenate([v4_ref[kp4, :], v4_ref[kc4, :]], axis=0),
               first_super if sub == 0 else never)
        hi, lo = divmod(i, R4)
        q16 = pl.ds(lo * ROWS4 + hi, nq, stride=R4)
        kp16 = pl.ds(prev0 + lo * ROWS4 + hi, nq, stride=R4)
        kc16 = pl.ds(cur0 + lo * ROWS4 + hi, nq, stride=R4)
        in16 = (q4_ref[q16, :],
                jnp.concatenate([k4_ref[kp16, :], k4_ref[kc16, :]], axis=0),
                jnp.concatenate([v4_ref[kp16, :], v4_ref[kc16, :]], axis=0),
                first_super)
        outs = [softmax_block(*args) for args in (in1, in4, in16)]
        for pi, qsl in enumerate((q1, q4, q16)):
            os_ref[pi, qsl, :] = outs[pi][0]
            ls_ref[pi, qsl, :] = outs[pi][1]
            ms_ref[pi, qsl, :] = outs[pi][2]

    rows = 256

    def combine(c, carry):
        home = pl.ds(pl.multiple_of(c * rows, rows), rows)
        r, part = c // (ROWS4 // rows), c % (ROWS4 // rows)
        tok = pl.ds(r + part * (rows * R4), rows, stride=R4)
        ms = [ms_ref[0, tok, :], ms_ref[1, home, :], ms_ref[2, home, :]]
        os = [os_ref[0, tok, :], os_ref[1, home, :], os_ref[2, home, :]]
        ls = [ls_ref[0, tok, :], ls_ref[1, home, :], ls_ref[2, home, :]]
        mx = jnp.maximum(jnp.maximum(ms[0], ms[1]), ms[2])
        num = jnp.zeros((rows, LANES), F32)
        den = jnp.zeros((rows, LANES), F32)
        for pi in range(len(DILATIONS)):
            e = jnp.exp2(ms[pi] - mx)
            num = num + e * os[pi]
            den = den + e * ls[pi]
        fin_ref[tok, :] = num / den
        return carry

    lax.fori_loop(0, SUPER // rows, combine, 0)

    def gate(c, carry):
        rs = pl.ds(pl.multiple_of(c * rows, rows), rows)
        o_ref[rs, :] = (_silu(zd_ref[rs, :].astype(F32)) * fin_ref[rs, :]).astype(o_ref.dtype)
        return carry

    lax.fori_loop(0, SUPER // rows, gate, 0)


def _attention4(h1, col0, batch, seq):
    t = batch * seq
    ns = seq // SUPER
    hw = D_HEAD_DIM
    cb0 = col0 // hw
    per = WIDTH // hw
    npat = len(DILATIONS)

    def cur(part):
        return lambda b, g, s: (b * ns + s, cb0 + part * per + g)

    blk = (SUPER, hw)
    return pl.pallas_call(
        _attn4_kernel,
        grid=(batch, per, ns),
        in_specs=[pl.BlockSpec(blk, cur(0)), pl.BlockSpec(blk, cur(1)), pl.BlockSpec(blk, cur(2)),
                  pl.BlockSpec(blk, cur(3))],
        out_specs=pl.BlockSpec(blk, lambda b, g, s: (b * ns + s, g)),
        out_shape=jax.ShapeDtypeStruct((t, WIDTH), BF16),
        scratch_shapes=[pltpu.VMEM((SUPER, LANES), F32),
                        pltpu.VMEM((2 * SUPER, LANES), F32),
                        pltpu.VMEM((2 * SUPER, LANES), F32),
                        pltpu.VMEM((SUPER, LANES), F32),
                        pltpu.VMEM((2 * SUPER, LANES), F32),
                        pltpu.VMEM((2 * SUPER, LANES), F32),
                        pltpu.VMEM((npat, SUPER, LANES), F32),
                        pltpu.VMEM((npat, SUPER, LANES), F32),
                        pltpu.VMEM((npat, SUPER, LANES), F32),
                        pltpu.VMEM((SUPER, LANES), F32),
                        pltpu.VMEM((2, N_BACK, 2 * N_BACK), F32)],
        compiler_params=_cparams("arbitrary", "arbitrary", "arbitrary"),
    )(h1, h1, h1, h1)


def kernel(x, even_norm_g, even_w_in, gmlp_ln_g, gmlp_ln_b, gmlp_ws, gmlp_bs, ssd_conv_w, ssd_conv_b,
           ssd_dt_bias, ssd_a_log, ssd_d, ssd_norm_g, even_w_out, odd_norm_g, odd_w_in, sconv_w,
           odd_w_out, final_norm_g):
    batch, seq, d = x.shape
    assert d == D_MODEL and seq % SUPER == 0
    assert even_norm_g.shape[0] == 1 and odd_norm_g.shape[0] == 1
    t = batch * seq
    x0 = x.reshape(t, d)
    w_in0_t = jnp.swapaxes(even_w_in, 1, 2).reshape(-1, d)
    w_out0 = even_w_out.reshape(-1, d)
    w_in1 = odd_w_in.reshape(d, -1)
    w_out1 = odd_w_out.reshape(-1, d)

    n_main = 4 * WIDTH + B_XBC
    w_dt = jnp.pad(w_in0_t[n_main:, :].T, ((0, 0), (0, LANES - B_HEADS)))
    xn0, dt_raw = _rmsnorm_dt(x0, even_norm_g[0], w_dt)
    h0 = _matmul(xn0, w_in0_t, n_main, w_is_nk=True)
    x1, xn1 = _mix0(h0, dt_raw, x0, _cast_bf16(w_out0), odd_norm_g[0], batch, seq,
                    gmlp_ln_g[0], gmlp_ln_b[0], gmlp_ws[0], gmlp_bs[0],
                    ssd_conv_w[0], ssd_conv_b[0], ssd_dt_bias[0], ssd_a_log[0], ssd_d[0], ssd_norm_g[0])

    h1 = _matmul(xn1, w_in1, w_in1.shape[1])
    yc = _shortconv(h1, sconv_w[0], batch, seq)
    yd = _attention4(h1, 4 * WIDTH, batch, seq)
    return _outproj_norm(yc, yd, _cast_bf16(w_out1), x1, final_norm_g).reshape(batch, seq, d)
```

```python
import functools

import jax
import jax.numpy as jnp
from jax import lax
from jax.experimental import pallas as pl
from jax.experimental.pallas import tpu as pltpu

F32 = jnp.float32
BF16 = jnp.bfloat16

EPS = 1e-5
D_MODEL = 2048
WIDTH = 2048
A_GROUPS = 8
CHUNK = 128
MIX0_CHUNKS = 1
B_HEAD_DIM = 64
B_HEADS = WIDTH // B_HEAD_DIM
B_GROUPS = 8
B_STATE = 128
B_CONV = 4
B_XBC = WIDTH + 2 * B_GROUPS * B_STATE
HEADS_PER_GROUP = B_HEADS // B_GROUPS
GROUP_W = WIDTH // B_GROUPS
C_CONV = 3
D_HEAD_DIM = 128
D_HEADS = WIDTH // D_HEAD_DIM
N_BACK = 128
DILATIONS = (1, 4, 16)
SUPER = N_BACK * DILATIONS[-1]
LANES = 128
SUBLANES = 8
VMEM_LIMIT = 56 * 1024 * 1024


def _cparams(*sem):
    return pltpu.CompilerParams(dimension_semantics=sem, vmem_limit_bytes=VMEM_LIMIT)


def _silu(z):
    hz = 0.5 * z
    return hz * jnp.tanh(hz) + hz


def _rmsnorm_dt_kernel(x_ref, g_ref, w_ref, o_ref, dt_ref):
    x = x_ref[...]
    ms = jnp.mean(x * x, axis=-1, keepdims=True)
    xn = (x * lax.rsqrt(ms + EPS) * g_ref[...]).astype(o_ref.dtype)
    o_ref[...] = xn
    dt_ref[...] = jnp.dot(xn, w_ref[...].astype(BF16), preferred_element_type=F32)


def _rmsnorm_dt(x2d, g, w_dt, tm=512):
    t, d = x2d.shape
    n = w_dt.shape[1]
    return pl.pallas_call(
        _rmsnorm_dt_kernel,
        grid=(t // tm,),
        in_specs=[pl.BlockSpec((tm, d), lambda i: (i, 0)),
                  pl.BlockSpec((1, d), lambda i: (0, 0)),
                  pl.BlockSpec((d, n), lambda i: (0, 0))],
        out_specs=[pl.BlockSpec((tm, d), lambda i: (i, 0)), pl.BlockSpec((tm, n), lambda i: (i, 0))],
        out_shape=[jax.ShapeDtypeStruct((t, d), BF16), jax.ShapeDtypeStruct((t, n), F32)],
        compiler_params=_cparams("arbitrary"),
    )(x2d, g.reshape(1, d), w_dt)


def _outproj_norm_kernel(a1_ref, a2_ref, w_ref, r_ref, g_ref, o_ref):
    k1 = a1_ref.shape[1]
    x = (jnp.dot(a1_ref[...], w_ref[0:k1, :], preferred_element_type=F32)
         + jnp.dot(a2_ref[...], w_ref[k1:, :], preferred_element_type=F32) + r_ref[...])
    ms = jnp.mean(x * x, axis=-1, keepdims=True)
    o_ref[...] = x * lax.rsqrt(ms + EPS) * g_ref[...]


def _outproj_norm(a1, a2, w_bf16, res, g, tm=512):
    m, k1 = a1.shape
    k2 = a2.shape[1]
    n = w_bf16.shape[1]
    return pl.pallas_call(
        _outproj_norm_kernel,
        grid=(m // tm,),
        in_specs=[pl.BlockSpec((tm, k1), lambda i: (i, 0)),
                  pl.BlockSpec((tm, k2), lambda i: (i, 0)),
                  pl.BlockSpec(w_bf16.shape, lambda i: (0, 0), pipeline_mode=pl.Buffered(1)),
                  pl.BlockSpec((tm, n), lambda i: (i, 0)),
                  pl.BlockSpec((1, n), lambda i: (0, 0))],
        out_specs=pl.BlockSpec((tm, n), lambda i: (i, 0)),
        out_shape=jax.ShapeDtypeStruct((m, n), F32),
        compiler_params=_cparams("arbitrary"),
    )(a1, a2, w_bf16, res, g.reshape(1, n))


def _matmul_kernel(a_ref, w_ref, o_ref, wb_ref, *, cast_rows, w_is_nk):
    @pl.when(pl.program_id(1) == 0)
    def _():
        def body(k, carry):
            r = pl.multiple_of(k * cast_rows, cast_rows)
            wb_ref[pl.ds(r, cast_rows), :] = w_ref[pl.ds(r, cast_rows), :].astype(BF16)
            return carry
        lax.fori_loop(0, w_ref.shape[0] // cast_rows, body, 0)

    dims = (((1,), (1,)), ((), ())) if w_is_nk else (((1,), (0,)), ((), ()))
    o_ref[...] = lax.dot_general(a_ref[...], wb_ref[...], dims, preferred_element_type=F32).astype(o_ref.dtype)


def _matmul(a, w, n_cols, w_is_nk=False, tm=2048, tn=1024):
    m, k = a.shape
    if w_is_nk:
        w_block = pl.BlockSpec((tn, k), lambda j, i: (j, 0))
    else:
        w_block = pl.BlockSpec((k, tn), lambda j, i: (0, j))
    return pl.pallas_call(
        functools.partial(_matmul_kernel, cast_rows=256, w_is_nk=w_is_nk),
        grid=(n_cols // tn, m // tm),
        in_specs=[pl.BlockSpec((tm, k), lambda j, i: (i, 0)), w_block],
        out_specs=pl.BlockSpec((tm, tn), lambda j, i: (i, j)),
        out_shape=jax.ShapeDtypeStruct((m, n_cols), BF16),
        scratch_shapes=[pltpu.VMEM(w_block.block_shape, BF16)],
        compiler_params=_cparams("arbitrary", "arbitrary"),
    )(a, w)


def _softplus(x):
    return jnp.maximum(x, 0.0) + jnp.log1p(jnp.exp(-jnp.abs(x)))


def _split3(x):
    hi = x.astype(BF16)
    r1 = x - hi.astype(F32)
    mid = r1.astype(BF16)
    lo = (r1 - mid.astype(F32)).astype(BF16)
    return hi, mid, lo


def _mix0_kernel(u_ref, v_ref, za_ref, zb_ref, xbc_ref, dt_ref, xres_ref, wout_ref, g1_ref,
                 lng_ref, lnb_ref, ws_ref, bst_ref, cw_ref, cbias_ref, dtb_ref, alog_ref, dexp_ref, ng_ref,
                 x1_ref, xn1_ref,
                 o_ref, yprev_ref, wsb_ref, xbuf_ref, xs_ref, bm_ref, cm_ref, y_ref, st_ref):
    q = CHUNK
    row = lax.broadcasted_iota(jnp.int32, (q, q), 0)
    col = lax.broadcasted_iota(jnp.int32, (q, q), 1)
    causal = col <= row

    @pl.when(pl.program_id(1) == 0)
    def _():
        o_ref[...] = jnp.zeros_like(o_ref)
        xbuf_ref[0:SUBLANES, :] = jnp.zeros((SUBLANES, B_XBC), F32)
        st_ref[...] = jnp.zeros_like(st_ref)
        for g in range(A_GROUPS):
            wsb_ref[g] = jnp.where(causal, ws_ref[g], 0.0).astype(BF16)

    yprev_ref[...] = o_ref[...]

    tril = jnp.where(causal, 1.0, 0.0).astype(BF16)
    for ci in range(o_ref.shape[0] // q):
        rs = slice(ci * q, (ci + 1) * q)
        _mix0_chunk(rs, causal, tril, u_ref, v_ref, za_ref, zb_ref, xbc_ref, dt_ref,
                    lng_ref, lnb_ref, bst_ref, cw_ref, cbias_ref, dtb_ref, alog_ref, dexp_ref, ng_ref,
                    o_ref, wsb_ref, xbuf_ref, xs_ref, bm_ref, cm_ref, y_ref, st_ref)
    rows = o_ref.shape[0]
    xbuf_ref[0:SUBLANES, :] = xbuf_ref[rows:rows + SUBLANES, :]

    ncol = 512
    ss = jnp.zeros((o_ref.shape[0], 1), F32)
    for nb in range(D_MODEL // ncol):
        cs = slice(nb * ncol, (nb + 1) * ncol)
        xc = jnp.dot(yprev_ref[...], wout_ref[:, cs], preferred_element_type=F32) + xres_ref[:, cs]
        x1_ref[:, cs] = xc
        ss = ss + jnp.sum(xc * xc, axis=-1, keepdims=True)
    rstd1 = lax.rsqrt(ss * (1.0 / D_MODEL) + EPS)
    for nb in range(D_MODEL // ncol):
        cs = slice(nb * ncol, (nb + 1) * ncol)
        xn1_ref[:, cs] = (x1_ref[:, cs] * rstd1 * g1_ref[:, cs]).astype(xn1_ref.dtype)


def _mix0_chunk(rs, causal, tril, u_ref, v_ref, za_ref, zb_ref, xbc_ref, dt_ref,
                lng_ref, lnb_ref, bst_ref, cw_ref, cbias_ref, dtb_ref, alog_ref, dexp_ref, ng_ref,
                o_ref, wsb_ref, xbuf_ref, xs_ref, bm_ref, cm_ref, y_ref, st_ref):
    q = CHUNK
    cwid = 256
    base = SUBLANES + rs.start
    for j in range(B_XBC // cwid):
        sl = slice(j * cwid, (j + 1) * cwid)
        xbuf_ref[base:base + q, sl] = xbc_ref[rs, sl].astype(F32)
        acc = cbias_ref[:, sl]
        for k in range(B_CONV):
            lo = base - (B_CONV - 1) + k
            acc = acc + xbuf_ref[lo:lo + q, sl] * cw_ref[k:k + 1, sl]
        act = _silu(acc)
        lo = j * cwid
        if lo < WIDTH:
            xs_ref[:, lo:lo + cwid] = act
        elif lo < WIDTH + B_GROUPS * B_STATE:
            bm_ref[:, lo - WIDTH:lo - WIDTH + cwid] = act
        else:
            off = lo - WIDTH - B_GROUPS * B_STATE
            cm_ref[:, off:off + cwid] = act

    dt = _softplus(dt_ref[rs, :] + dtb_ref[...])
    adt = dt * (-jnp.exp(alog_ref[...]))
    a_cs = sum(jnp.dot(tril, part, preferred_element_type=F32) for part in _split3(adt))
    a_cs_t = a_cs.T
    dt_t = dt.T

    for g in range(B_GROUPS):
        gs = slice(g * B_STATE, (g + 1) * B_STATE)
        bg = bm_ref[:, gs]
        cgb = cm_ref[:, gs].astype(BF16)
        bg_t = bg.T
        cb = lax.dot_general(cgb, bg.astype(BF16), (((1,), (1,)), ((), ())),
                             preferred_element_type=F32)
        for r in range(HEADS_PER_GROUP):
            h = g * HEADS_PER_GROUP + r
            hs = slice(h * B_HEAD_DIM, (h + 1) * B_HEAD_DIM)
            colv = a_cs[:, h:h + 1]
            rowv = a_cs_t[h:h + 1, :]
            dtr = dt_t[h:h + 1, :]
            decay = jnp.exp(jnp.where(causal, colv - rowv, -jnp.inf))
            mh = (cb * decay * dtr).astype(BF16)
            xh = xs_ref[:, hs]
            xhb = xh.astype(BF16)
            state = st_ref[h]
            y = jnp.dot(mh, xhb, preferred_element_type=F32)
            y = y + jnp.exp(colv) * jnp.dot(cgb, state.astype(BF16), preferred_element_type=F32)
            y_ref[:, hs] = y + dexp_ref[:, hs] * xh
            last = a_cs_t[h:h + 1, q - 1:q]
            wrow = jnp.exp(last - rowv) * dtr
            new = jnp.dot((bg_t * wrow).astype(BF16), xhb, preferred_element_type=F32)
            st_ref[h] = state * jnp.exp(last) + new

    for g in range(B_GROUPS):
        sl = slice(g * GROUP_W, (g + 1) * GROUP_W)
        y = y_ref[:, sl] * _silu(zb_ref[rs, sl].astype(F32))
        ms = jnp.mean(y * y, axis=-1, keepdims=True)
        o_ref[rs, WIDTH + g * GROUP_W:WIDTH + (g + 1) * GROUP_W] = (
            y * lax.rsqrt(ms + EPS) * ng_ref[:, sl]).astype(o_ref.dtype)

    v = v_ref[rs, :].astype(F32)
    mu = jnp.mean(v, axis=-1, keepdims=True)
    xc = v - mu
    var = jnp.mean(xc * xc, axis=-1, keepdims=True)
    rstd = lax.rsqrt(var + EPS)
    for g in range(A_GROUPS):
        sl = slice(g * GROUP_W, (g + 1) * GROUP_W)
        vg = v_ref[rs, sl].astype(F32)
        vn = ((vg - mu) * rstd * lng_ref[:, sl] + lnb_ref[:, sl]).astype(BF16)
        mixed = jnp.dot(wsb_ref[g], vn, preferred_element_type=F32) + bst_ref[:, g:g + 1]
        z = za_ref[rs, sl].astype(F32)
        u = u_ref[rs, sl].astype(F32)
        o_ref[rs, sl] = (_silu(z) * (u * mixed)).astype(o_ref.dtype)


def _mix0(h0, dt_raw, x0, w_out, next_norm_g, batch, seq,
          lng, lnb, ws, bs, conv_w, conv_b, dt_bias, a_log, d_skip, norm_g):
    t = batch * seq
    rows = MIX0_CHUNKS * CHUNK
    nc = seq // rows
    pad = LANES - B_HEADS
    row = lambda a: a.reshape(1, -1)
    params = [row(lng), row(lnb), ws, bs.T, conv_w, row(conv_b),
              row(jnp.pad(dt_bias, (0, pad))), row(jnp.pad(a_log, (0, pad))),
              row(jnp.repeat(d_skip, B_HEAD_DIM)), row(norm_g)]
    tok = lambda cb: (lambda b, c: (b * nc + jnp.minimum(c, nc - 1), cb))
    lag = lambda b, c: (b * nc + jnp.maximum(c - 1, 0), 0)
    full = lambda a: pl.BlockSpec(a.shape, lambda b, c: (0,) * a.ndim)
    in_specs = [pl.BlockSpec((rows, WIDTH), tok(0)),
                pl.BlockSpec((rows, WIDTH), tok(1)),
                pl.BlockSpec((rows, WIDTH), tok(2)),
                pl.BlockSpec((rows, WIDTH), tok(3)),
                pl.BlockSpec((rows, B_XBC), tok(2)),
                pl.BlockSpec((rows, LANES), tok(0)),
                pl.BlockSpec((rows, D_MODEL), lag),
                pl.BlockSpec(w_out.shape, lambda b, c: (0, 0), pipeline_mode=pl.Buffered(1)),
                full(row(next_norm_g))]
    in_specs += [full(p) for p in params]
    return pl.pallas_call(
        _mix0_kernel,
        grid=(batch, nc + 1),
        in_specs=in_specs,
        out_specs=[pl.BlockSpec((rows, D_MODEL), lag), pl.BlockSpec((rows, D_MODEL), lag)],
        out_shape=[jax.ShapeDtypeStruct((t, D_MODEL), F32), jax.ShapeDtypeStruct((t, D_MODEL), BF16)],
        scratch_shapes=[pltpu.VMEM((rows, 2 * WIDTH), BF16),
                        pltpu.VMEM((rows, 2 * WIDTH), BF16),
                        pltpu.VMEM((A_GROUPS, CHUNK, CHUNK), BF16),
                        pltpu.VMEM((SUBLANES + rows, B_XBC), F32),
                        pltpu.VMEM((CHUNK, WIDTH), F32),
                        pltpu.VMEM((CHUNK, B_GROUPS * B_STATE), F32),
                        pltpu.VMEM((CHUNK, B_GROUPS * B_STATE), F32),
                        pltpu.VMEM((CHUNK, WIDTH), F32),
                        pltpu.VMEM((B_HEADS, B_STATE, B_HEAD_DIM), F32)],
        compiler_params=_cparams("arbitrary", "arbitrary"),
    )(h0, h0, h0, h0, h0, dt_raw, x0, w_out, row(next_norm_g), *params)


def _cast_kernel(w_ref, o_ref):
    o_ref[...] = w_ref[...].astype(o_ref.dtype)


def _cast_bf16(w, tr=512):
    r, c = w.shape
    return pl.pallas_call(
        _cast_kernel,
        grid=(r // tr,),
        in_specs=[pl.BlockSpec((tr, c), lambda i: (i, 0))],
        out_specs=pl.BlockSpec((tr, c), lambda i: (i, 0)),
        out_shape=jax.ShapeDtypeStruct((r, c), BF16),
        compiler_params=_cparams("arbitrary"),
    )(w)


def _shortconv_kernel(bg_ref, cg_ref, hx_ref, zc_ref, cw_ref, o_ref, xbuf_ref):
    tile = o_ref.shape[0]

    @pl.when(pl.program_id(1) == 0)
    def _():
        xbuf_ref[0:SUBLANES, :] = jnp.zeros((SUBLANES, WIDTH), F32)

    cwid, rows = 256, 128
    for j in range(WIDTH // cwid):
        sl = slice(j * cwid, (j + 1) * cwid)
        for i in range(tile // rows):
            r0 = i * rows
            xbuf_ref[SUBLANES + r0:SUBLANES + r0 + rows, sl] = (
                cg_ref[r0:r0 + rows, sl].astype(F32) * hx_ref[r0:r0 + rows, sl].astype(F32))
        for i in range(tile // rows):
            r0 = i * rows
            acc = None
            for k in range(C_CONV):
                lo = SUBLANES - (C_CONV - 1) + k + r0
                term = xbuf_ref[lo:lo + rows, sl] * cw_ref[k:k + 1, sl]
                acc = term if acc is None else acc + term
            gate = _silu(zc_ref[r0:r0 + rows, sl].astype(F32)) * bg_ref[r0:r0 + rows, sl].astype(F32)
            o_ref[r0:r0 + rows, sl] = (gate * acc).astype(o_ref.dtype)
        xbuf_ref[0:SUBLANES, sl] = xbuf_ref[tile:tile + SUBLANES, sl]


def _shortconv(h1, conv_w, batch, seq, tile=512):
    t = batch * seq
    nt = seq // tile
    tok = lambda cb: (lambda b, i: (b * nt + i, cb))
    return pl.pallas_call(
        _shortconv_kernel,
        grid=(batch, nt),
        in_specs=[pl.BlockSpec((tile, WIDTH), tok(0)), pl.BlockSpec((tile, WIDTH), tok(1)),
                  pl.BlockSpec((tile, WIDTH), tok(2)), pl.BlockSpec((tile, WIDTH), tok(3)),
                  pl.BlockSpec(conv_w.shape, lambda b, i: (0, 0))],
        out_specs=pl.BlockSpec((tile, WIDTH), tok(0)),
        out_shape=jax.ShapeDtypeStruct((t, WIDTH), BF16),
        scratch_shapes=[pltpu.VMEM((SUBLANES + tile, WIDTH), F32)],
        compiler_params=_cparams("arbitrary", "arbitrary"),
    )(h1, h1, h1, h1, conv_w)


R4 = DILATIONS[1]
ROWS4 = SUPER // R4


def _attn4_kernel(q_ref, kc_ref, vc_ref, zd_ref, o_ref,
                  qn_ref, kn_ref, vn_ref, q4_ref, k4_ref, v4_ref, os_ref, ls_ref, ms_ref, fin_ref, bias_ref):
    first_super = pl.program_id(2) == 0
    cur0 = pl.multiple_of((pl.program_id(2) % 2) * SUPER, SUPER)
    prev0 = pl.multiple_of(SUPER - cur0, SUPER)
    nq = N_BACK
    blocks = SUPER // nq
    a = lax.broadcasted_iota(jnp.int32, (nq, 2 * nq), 0)
    j = lax.broadcasted_iota(jnp.int32, (nq, 2 * nq), 1)
    in_band = jnp.logical_and(j >= a, j <= a + nq)
    bias_ref[0] = jnp.where(in_band, 0.0, -jnp.inf)
    bias_ref[1] = jnp.where(jnp.logical_and(in_band, j >= nq), 0.0, -jnp.inf)
    qscale = D_HEAD_DIM ** -0.5 * 1.4426950408889634
    ones_cols = jnp.ones((2 * nq, LANES), BF16)

    @pl.when(first_super)
    def _():
        for ref in (kn_ref, vn_ref, k4_ref, v4_ref):
            ref[pl.ds(prev0, SUPER), :] = jnp.zeros((SUPER, LANES), F32)

    qn_ref[...] = q_ref[...].astype(F32) * qscale
    kn_ref[pl.ds(cur0, SUPER), :] = kc_ref[...].astype(F32)
    vn_ref[pl.ds(cur0, SUPER), :] = vc_ref[...].astype(F32)
    for r in range(R4):
        q4_ref[r * ROWS4:(r + 1) * ROWS4] = qn_ref[pl.ds(r, ROWS4, stride=R4), :]
        dst = pl.ds(pl.multiple_of(cur0 + r * ROWS4, ROWS4), ROWS4)
        k4_ref[dst, :] = kn_ref[pl.ds(cur0 + r, ROWS4, stride=R4), :]
        v4_ref[dst, :] = vn_ref[pl.ds(cur0 + r, ROWS4, stride=R4), :]

    def softmax_block(qf, kf, vf, no_prev):
        bias = bias_ref[0] if no_prev is None else bias_ref[no_prev.astype(jnp.int32)]
        s = lax.dot_general(qf.astype(BF16), kf.astype(BF16), (((1,), (1,)), ((), ())),
                            preferred_element_type=F32) + bias
        m = jnp.max(s, axis=-1, keepdims=True)
        p = jnp.exp2(s - m).astype(BF16)
        o2 = jnp.dot(p, jnp.concatenate([vf.astype(BF16), ones_cols], axis=1), preferred_element_type=F32)
        return o2[:, :LANES], o2[:, LANES:], jnp.broadcast_to(m, (nq, LANES))

    def rows_of(base, off):
        return pl.ds(pl.multiple_of(base + off, nq), nq)

    never = None
    for i in range(blocks):
        q1 = pl.ds(i * nq, nq)
        kc1 = rows_of(cur0, i * nq)
        kp1 = rows_of(cur0, (i - 1) * nq) if i > 0 else rows_of(prev0, SUPER - nq)
        in1 = (qn_ref[q1, :],
               jnp.concatenate([kn_ref[kp1, :], kn_ref[kc1, :]], axis=0),
               jnp.concatenate([vn_ref[kp1, :], vn_ref[kc1, :]], axis=0),
               first_super if i == 0 else never)
        res, sub = divmod(i, ROWS4 // nq)
        q4 = pl.ds(res * ROWS4 + sub * nq, nq)
        kc4 = rows_of(cur0, res * ROWS4 + sub * nq)
        kp4 = rows_of(cur0, res * ROWS4 + (sub - 1) * nq) if sub > 0 else rows_of(prev0, res * ROWS4 + ROWS4 - nq)
        in4 = (q4_ref[q4, :],
               jnp.concatenate([k4_ref[kp4, :], k4_ref[kc4, :]], axis=0),
               jnp.concatenate([v4_ref[kp4, :], v4_ref[kc4, :]], axis=0),
               first_super if sub == 0 else never)
        hi, lo = divmod(i, R4)
        q16 = pl.ds(lo * ROWS4 + hi, nq, stride=R4)
        kp16 = pl.ds(prev0 + lo * ROWS4 + hi, nq, stride=R4)
        kc16 = pl.ds(cur0 + lo * ROWS4 + hi, nq, stride=R4)
        in16 = (q4_ref[q16, :],
                jnp.concatenate([k4_ref[kp16, :], k4_ref[kc16, :]], axis=0),
                jnp.concatenate([v4_ref[kp16, :], v4_ref[kc16, :]], axis=0),
                first_super)
        outs = [softmax_block(*args) for args in (in1, in4, in16)]
        for pi, qsl in enumerate((q1, q4, q16)):
            os_ref[pi, qsl, :] = outs[pi][0]
            ls_ref[pi, qsl, :] = outs[pi][1]
            ms_ref[pi, qsl, :] = outs[pi][2]

    rows = 256

    def combine(c, carry):
        home = pl.ds(pl.multiple_of(c * rows, rows), rows)
        r, part = c // (ROWS4 // rows), c % (ROWS4 // rows)
        tok = pl.ds(r + part * (rows * R4), rows, stride=R4)
        ms = [ms_ref[0, tok, :], ms_ref[1, home, :], ms_ref[2, home, :]]
        os = [os_ref[0, tok, :], os_ref[1, home, :], os_ref[2, home, :]]
        ls = [ls_ref[0, tok, :], ls_ref[1, home, :], ls_ref[2, home, :]]
        mx = jnp.maximum(jnp.maximum(ms[0], ms[1]), ms[2])
        num = jnp.zeros((rows, LANES), F32)
        den = jnp.zeros((rows, LANES), F32)
        for pi in range(len(DILATIONS)):
            e = jnp.exp2(ms[pi] - mx)
            num = num + e * os[pi]
            den = den + e * ls[pi]
        fin_ref[tok, :] = num / den
        return carry

    lax.fori_loop(0, SUPER // rows, combine, 0)

    def gate(c, carry):
        rs = pl.ds(pl.multiple_of(c * rows, rows), rows)
        o_ref[rs, :] = (_silu(zd_ref[rs, :].astype(F32)) * fin_ref[rs, :]).astype(o_ref.dtype)
        return carry

    lax.fori_loop(0, SUPER // rows, gate, 0)


def _attention4(h1, col0, batch, seq):
    t = batch * seq
    ns = seq // SUPER
    hw = D_HEAD_DIM
    cb0 = col0 // hw
    per = WIDTH // hw
    npat = len(DILATIONS)

    def cur(part):
        return lambda b, g, s: (b * ns + s, cb0 + part * per + g)

    blk = (SUPER, hw)
    return pl.pallas_call(
        _attn4_kernel,
        grid=(batch, per, ns),
        in_specs=[pl.BlockSpec(blk, cur(0)), pl.BlockSpec(blk, cur(1)), pl.BlockSpec(blk, cur(2)),
                  pl.BlockSpec(blk, cur(3))],
        out_specs=pl.BlockSpec(blk, lambda b, g, s: (b * ns + s, g)),
        out_shape=jax.ShapeDtypeStruct((t, WIDTH), BF16),
        scratch_shapes=[pltpu.VMEM((SUPER, LANES), F32),
                        pltpu.VMEM((2 * SUPER, LANES), F32),
                        pltpu.VMEM((2 * SUPER, LANES), F32),
                        pltpu.VMEM((SUPER, LANES), F32),
                        pltpu.VMEM((2 * SUPER, LANES), F32),
                        pltpu.VMEM((2 * SUPER, LANES), F32),
                        pltpu.VMEM((npat, SUPER, LANES), F32),
                        pltpu.VMEM((npat, SUPER, LANES), F32),
                        pltpu.VMEM((npat, SUPER, LANES), F32),
                        pltpu.VMEM((SUPER, LANES), F32),
                        pltpu.VMEM((2, N_BACK, 2 * N_BACK), F32)],
        compiler_params=_cparams("arbitrary", "arbitrary", "arbitrary"),
    )(h1, h1, h1, h1)


def kernel(x, even_norm_g, even_w_in, gmlp_ln_g, gmlp_ln_b, gmlp_ws, gmlp_bs, ssd_conv_w, ssd_conv_b,
           ssd_dt_bias, ssd_a_log, ssd_d, ssd_norm_g, even_w_out, odd_norm_g, odd_w_in, sconv_w,
           odd_w_out, final_norm_g):
    batch, seq, d = x.shape
    assert d == D_MODEL and seq % SUPER == 0
    assert even_norm_g.shape[0] == 1 and odd_norm_g.shape[0] == 1
    t = batch * seq
    x0 = x.reshape(t, d)
    w_in0_t = jnp.swapaxes(even_w_in, 1, 2).reshape(-1, d)
    w_out0 = even_w_out.reshape(-1, d)
    w_in1 = odd_w_in.reshape(d, -1)
    w_out1 = odd_w_out.reshape(-1, d)

    n_main = 4 * WIDTH + B_XBC
    w_dt = jnp.pad(w_in0_t[n_main:, :].T, ((0, 0), (0, LANES - B_HEADS)))
    xn0, dt_raw = _rmsnorm_dt(x0, even_norm_g[0], w_dt)
    h0 = _matmul(xn0, w_in0_t, n_main, w_is_nk=True)
    x1, xn1 = _mix0(h0, dt_raw, x0, _cast_bf16(w_out0), odd_norm_g[0], batch, seq,
                    gmlp_ln_g[0], gmlp_ln_b[0], gmlp_ws[0], gmlp_bs[0],
                    ssd_conv_w[0], ssd_conv_b[0], ssd_dt_bias[0], ssd_a_log[0], ssd_d[0], ssd_norm_g[0])

    h1 = _matmul(xn1, w_in1, w_in1.shape[1])
    yc = _shortconv(h1, sconv_w[0], batch, seq)
    yd = _attention4(h1, 4 * WIDTH, batch, seq)
    return _outproj_norm(yc, yd, _cast_bf16(w_out1), x1, final_norm_g).reshape(batch, seq, d)
```

```python
import functools

import jax
import jax.numpy as jnp
from jax import lax
from jax.experimental import pallas as pl
from jax.experimental.pallas import tpu as pltpu

F32 = jnp.float32
BF16 = jnp.bfloat16

EPS = 1e-5
D_MODEL = 2048
WIDTH = 2048
A_GROUPS = 8
CHUNK = 128
MIX0_CHUNKS = 1
B_HEAD_DIM = 64
B_HEADS = WIDTH // B_HEAD_DIM
B_GROUPS = 8
B_STATE = 128
B_CONV = 4
B_XBC = WIDTH + 2 * B_GROUPS * B_STATE
HEADS_PER_GROUP = B_HEADS // B_GROUPS
GROUP_W = WIDTH // B_GROUPS
C_CONV = 3
D_HEAD_DIM = 128
D_HEADS = WIDTH // D_HEAD_DIM
N_BACK = 128
DILATIONS = (1, 4, 16)
SUPER = N_BACK * DILATIONS[-1]
LANES = 128
SUBLANES = 8
VMEM_LIMIT = 56 * 1024 * 1024


def _cparams(*sem):
    return pltpu.CompilerParams(dimension_semantics=sem, vmem_limit_bytes=VMEM_LIMIT)


def _silu(z):
    hz = 0.5 * z
    return hz * jnp.tanh(hz) + hz


def _rmsnorm_dt_kernel(x_ref, g_ref, w_ref, o_ref, dt_ref):
    x = x_ref[...]
    ms = jnp.mean(x * x, axis=-1, keepdims=True)
    xn = (x * lax.rsqrt(ms + EPS) * g_ref[...]).astype(o_ref.dtype)
    o_ref[...] = xn
    dt_ref[...] = jnp.dot(xn, w_ref[...].astype(BF16), preferred_element_type=F32)


def _rmsnorm_dt(x2d, g, w_dt, tm=512):
    t, d = x2d.shape
    n = w_dt.shape[1]
    return pl.pallas_call(
        _rmsnorm_dt_kernel,
        grid=(t // tm,),
        in_specs=[pl.BlockSpec((tm, d), lambda i: (i, 0)),
                  pl.BlockSpec((1, d), lambda i: (0, 0)),
                  pl.BlockSpec((d, n), lambda i: (0, 0))],
        out_specs=[pl.BlockSpec((tm, d), lambda i: (i, 0)), pl.BlockSpec((tm, n), lambda i: (i, 0))],
        out_shape=[jax.ShapeDtypeStruct((t, d), BF16), jax.ShapeDtypeStruct((t, n), F32)],
        compiler_params=_cparams("arbitrary"),
    )(x2d, g.reshape(1, d), w_dt)


def _outproj_norm_kernel(a1_ref, a2_ref, w_ref, r_ref, g_ref, o_ref):
    k1 = a1_ref.shape[1]
    x = (jnp.dot(a1_ref[...], w_ref[0:k1, :], preferred_element_type=F32)
         + jnp.dot(a2_ref[...], w_ref[k1:, :], preferred_element_type=F32) + r_ref[...])
    ms = jnp.mean(x * x, axis=-1, keepdims=True)
    o_ref[...] = x * lax.rsqrt(ms + EPS) * g_ref[...]


def _outproj_norm(a1, a2, w_bf16, res, g, tm=512):
    m, k1 = a1.shape
    k2 = a2.shape[1]
    n = w_bf16.shape[1]
    return pl.pallas_call(
        _outproj_norm_kernel,
        grid=(m // tm,),
        in_specs=[pl.BlockSpec((tm, k1), lambda i: (i, 0)),
                  pl.BlockSpec((tm, k2), lambda i: (i, 0)),
                  pl.BlockSpec(w_bf16.shape, lambda i: (0, 0), pipeline_mode=pl.Buffered(1)),
                  pl.BlockSpec((tm, n), lambda i: (i, 0)),
                  pl.BlockSpec((1, n), lambda i: (0, 0))],
        out_specs=pl.BlockSpec((tm, n), lambda i: (i, 0)),
        out_shape=jax.ShapeDtypeStruct((m, n), F32),
        compiler_params=_cparams("arbitrary"),
    )(a1, a2, w_bf16, res, g.reshape(1, n))


def _matmul_kernel(a_ref, w_ref, o_ref, wb_ref, *, cast_rows, w_is_nk):
    @pl.when(pl.program_id(1) == 0)
    def _():
        def body(k, carry):
            r = pl.multiple_of(k * cast_rows, cast_rows)
            wb_ref[pl.ds(r, cast_rows), :] = w_ref[pl.ds(r, cast_rows), :].astype(BF16)
            return carry
        lax.fori_loop(0, w_ref.shape[0] // cast_rows, body, 0)

    dims = (((1,), (1,)), ((), ())) if w_is_nk else (((1,), (0,)), ((), ()))
    o_ref[...] = lax.dot_general(a_ref[...], wb_ref[...], dims, preferred_element_type=F32).astype(o_ref.dtype)


def _matmul(a, w, n_cols, w_is_nk=False, tm=2048, tn=1024):
    m, k = a.shape
    if w_is_nk:
        w_block = pl.BlockSpec((tn, k), lambda j, i: (j, 0))
    else:
        w_block = pl.BlockSpec((k, tn), lambda j, i: (0, j))
    return pl.pallas_call(
        functools.partial(_matmul_kernel, cast_rows=256, w_is_nk=w_is_nk),
        grid=(n_cols // tn, m // tm),
        in_specs=[pl.BlockSpec((tm, k), lambda j, i: (i, 0)), w_block],
        out_specs=pl.BlockSpec((tm, tn), lambda j, i: (i, j)),
        out_shape=jax.ShapeDtypeStruct((m, n_cols), BF16),
        scratch_shapes=[pltpu.VMEM(w_block.block_shape, BF16)],
        compiler_params=_cparams("arbitrary", "arbitrary"),
    )(a, w)


def _softplus(x):
    return jnp.maximum(x, 0.0) + jnp.log1p(jnp.exp(-jnp.abs(x)))


def _split3(x):
    hi = x.astype(BF16)
    r1 = x - hi.astype(F32)
    mid = r1.astype(BF16)
    lo = (r1 - mid.astype(F32)).astype(BF16)
    return hi, mid, lo


def _mix0_kernel(u_ref, v_ref, za_ref, zb_ref, xbc_ref, dt_ref, xres_ref, wout_ref, g1_ref,
                 lng_ref, lnb_ref, ws_ref, bst_ref, cw_ref, cbias_ref, dtb_ref, alog_ref, dexp_ref, ng_ref,
                 x1_ref, xn1_ref,
                 o_ref, yprev_ref, wsb_ref, xbuf_ref, xs_ref, bm_ref, cm_ref, y_ref, st_ref):
    q = CHUNK
    row = lax.broadcasted_iota(jnp.int32, (q, q), 0)
    col = lax.broadcasted_iota(jnp.int32, (q, q), 1)
    causal = col <= row

    @pl.when(pl.program_id(1) == 0)
    def _():
        o_ref[...] = jnp.zeros_like(o_ref)
        xbuf_ref[0:SUBLANES, :] = jnp.zeros((SUBLANES, B_XBC), F32)
        st_ref[...] = jnp.zeros_like(st_ref)
        for g in range(A_GROUPS):
            wsb_ref[g] = jnp.where(causal, ws_ref[g], 0.0).astype(BF16)

    yprev_ref[...] = o_ref[...]

    tril = jnp.where(causal, 1.0, 0.0).astype(BF16)
    chunk_args = (causal, tril, u_ref, v_ref, za_ref, zb_ref, xbc_ref, dt_ref,
                  lng_ref, lnb_ref, bst_ref, cw_ref, cbias_ref, dtb_ref, alog_ref, dexp_ref, ng_ref,
                  o_ref, wsb_ref, xbuf_ref, xs_ref, bm_ref, cm_ref, y_ref, st_ref)
    chunks = [slice(ci * q, (ci + 1) * q) for ci in range(o_ref.shape[0] // q)]
    for rs in chunks:
        _mix0_chunk(rs, "conv", *chunk_args)
        _mix0_chunk(rs, "scan", *chunk_args)
    rows = o_ref.shape[0]
    xbuf_ref[0:SUBLANES, :] = xbuf_ref[rows:rows + SUBLANES, :]

    ncol = 512
    ss = jnp.zeros((o_ref.shape[0], 1), F32)
    for nb in range(D_MODEL // ncol):
        cs = slice(nb * ncol, (nb + 1) * ncol)
        xc = jnp.dot(yprev_ref[...], wout_ref[:, cs], preferred_element_type=F32) + xres_ref[:, cs]
        x1_ref[:, cs] = xc
        ss = ss + jnp.sum(xc * xc, axis=-1, keepdims=True)
    rstd1 = lax.rsqrt(ss * (1.0 / D_MODEL) + EPS)
    for nb in range(D_MODEL // ncol):
        cs = slice(nb * ncol, (nb + 1) * ncol)
        xn1_ref[:, cs] = (x1_ref[:, cs] * rstd1 * g1_ref[:, cs]).astype(xn1_ref.dtype)

    for rs in chunks:
        _mix0_chunk(rs, "gmlp", *chunk_args)


def _mix0_chunk(rs, part, causal, tril, u_ref, v_ref, za_ref, zb_ref, xbc_ref, dt_ref,
                lng_ref, lnb_ref, bst_ref, cw_ref, cbias_ref, dtb_ref, alog_ref, dexp_ref, ng_ref,
                o_ref, wsb_ref, xbuf_ref, xs_ref, bm_ref, cm_ref, y_ref, st_ref):
    q = CHUNK
    if part == "gmlp":
        _gmlp_chunk(rs, u_ref, v_ref, za_ref, lng_ref, lnb_ref, bst_ref, o_ref, wsb_ref)
        return
    if part == "scan":
        _scan_chunk(rs, causal, tril, zb_ref, dt_ref, dtb_ref, alog_ref, dexp_ref, ng_ref,
                    o_ref, xs_ref, bm_ref, cm_ref, y_ref, st_ref)
        return
    cwid = 256
    base = SUBLANES + rs.start
    for j in range(B_XBC // cwid):
        sl = slice(j * cwid, (j + 1) * cwid)
        xbuf_ref[base:base + q, sl] = xbc_ref[rs, sl].astype(F32)
        acc = cbias_ref[:, sl]
        for k in range(B_CONV):
            lo = base - (B_CONV - 1) + k
            acc = acc + xbuf_ref[lo:lo + q, sl] * cw_ref[k:k + 1, sl]
        act = _silu(acc)
        lo = j * cwid
        if lo < WIDTH:
            xs_ref[:, lo:lo + cwid] = act
        elif lo < WIDTH + B_GROUPS * B_STATE:
            bm_ref[:, lo - WIDTH:lo - WIDTH + cwid] = act
        else:
            off = lo - WIDTH - B_GROUPS * B_STATE
            cm_ref[:, off:off + cwid] = act


def _scan_chunk(rs, causal, tril, zb_ref, dt_ref, dtb_ref, alog_ref, dexp_ref, ng_ref,
                o_ref, xs_ref, bm_ref, cm_ref, y_ref, st_ref):
    q = CHUNK
    dt = _softplus(dt_ref[rs, :] + dtb_ref[...])
    adt = dt * (-jnp.exp(alog_ref[...]))
    a_cs = sum(jnp.dot(tril, part, preferred_element_type=F32) for part in _split3(adt))
    a_cs_t = a_cs.T
    dt_t = dt.T

    for g in range(B_GROUPS):
        gs = slice(g * B_STATE, (g + 1) * B_STATE)
        bg = bm_ref[:, gs]
        cgb = cm_ref[:, gs].astype(BF16)
        bg_t = bg.T
        cb = lax.dot_general(cgb, bg.astype(BF16), (((1,), (1,)), ((), ())),
                             preferred_element_type=F32)
        for r in range(HEADS_PER_GROUP):
            h = g * HEADS_PER_GROUP + r
            hs = slice(h * B_HEAD_DIM, (h + 1) * B_HEAD_DIM)
            colv = a_cs[:, h:h + 1]
            rowv = a_cs_t[h:h + 1, :]
            dtr = dt_t[h:h + 1, :]
            decay = jnp.exp(jnp.where(causal, colv - rowv, -jnp.inf))
            mh = (cb * decay * dtr).astype(BF16)
            xh = xs_ref[:, hs]
            xhb = xh.astype(BF16)
            state = st_ref[h]
            y = jnp.dot(mh, xhb, preferred_element_type=F32)
            y = y + jnp.exp(colv) * jnp.dot(cgb, state.astype(BF16), preferred_element_type=F32)
            y_ref[:, hs] = y + dexp_ref[:, hs] * xh
            last = a_cs_t[h:h + 1, q - 1:q]
            wrow = jnp.exp(last - rowv) * dtr
            new = jnp.dot((bg_t * wrow).astype(BF16), xhb, preferred_element_type=F32)
            st_ref[h] = state * jnp.exp(last) + new

    for g in range(B_GROUPS):
        sl = slice(g * GROUP_W, (g + 1) * GROUP_W)
        y = y_ref[:, sl] * _silu(zb_ref[rs, sl].astype(F32))
        ms = jnp.mean(y * y, axis=-1, keepdims=True)
        o_ref[rs, WIDTH + g * GROUP_W:WIDTH + (g + 1) * GROUP_W] = (
            y * lax.rsqrt(ms + EPS) * ng_ref[:, sl]).astype(o_ref.dtype)


def _gmlp_chunk(rs, u_ref, v_ref, za_ref, lng_ref, lnb_ref, bst_ref, o_ref, wsb_ref):
    v = v_ref[rs, :].astype(F32)
    mu = jnp.mean(v, axis=-1, keepdims=True)
    xc = v - mu
    var = jnp.mean(xc * xc, axis=-1, keepdims=True)
    rstd = lax.rsqrt(var + EPS)
    for g in range(A_GROUPS):
        sl = slice(g * GROUP_W, (g + 1) * GROUP_W)
        vg = v_ref[rs, sl].astype(F32)
        vn = ((vg - mu) * rstd * lng_ref[:, sl] + lnb_ref[:, sl]).astype(BF16)
        mixed = jnp.dot(wsb_ref[g], vn, preferred_element_type=F32) + bst_ref[:, g:g + 1]
        z = za_ref[rs, sl].astype(F32)
        u = u_ref[rs, sl].astype(F32)
        o_ref[rs, sl] = (_silu(z) * (u * mixed)).astype(o_ref.dtype)


def _mix0(h0, dt_raw, x0, w_out, next_norm_g, batch, seq,
          lng, lnb, ws, bs, conv_w, conv_b, dt_bias, a_log, d_skip, norm_g):
    t = batch * seq
    rows = MIX0_CHUNKS * CHUNK
    nc = seq // rows
    pad = LANES - B_HEADS
    row = lambda a: a.reshape(1, -1)
    params = [row(lng), row(lnb), ws, bs.T, conv_w, row(conv_b),
              row(jnp.pad(dt_bias, (0, pad))), row(jnp.pad(a_log, (0, pad))),
              row(jnp.repeat(d_skip, B_HEAD_DIM)), row(norm_g)]
    tok = lambda cb: (lambda b, c: (b * nc + jnp.minimum(c, nc - 1), cb))
    lag = lambda b, c: (b * nc + jnp.maximum(c - 1, 0), 0)
    full = lambda a: pl.BlockSpec(a.shape, lambda b, c: (0,) * a.ndim)
    in_specs = [pl.BlockSpec((rows, WIDTH), tok(0)),
                pl.BlockSpec((rows, WIDTH), tok(1)),
                pl.BlockSpec((rows, WIDTH), tok(2)),
                pl.BlockSpec((rows, WIDTH), tok(3)),
                pl.BlockSpec((rows, B_XBC), tok(2)),
                pl.BlockSpec((rows, LANES), tok(0)),
                pl.BlockSpec((rows, D_MODEL), lag),
                pl.BlockSpec(w_out.shape, lambda b, c: (0, 0), pipeline_mode=pl.Buffered(1)),
                full(row(next_norm_g))]
    in_specs += [full(p) for p in params]
    return pl.pallas_call(
        _mix0_kernel,
        grid=(batch, nc + 1),
        in_specs=in_specs,
        out_specs=[pl.BlockSpec((rows, D_MODEL), lag), pl.BlockSpec((rows, D_MODEL), lag)],
        out_shape=[jax.ShapeDtypeStruct((t, D_MODEL), F32), jax.ShapeDtypeStruct((t, D_MODEL), BF16)],
        scratch_shapes=[pltpu.VMEM((rows, 2 * WIDTH), BF16),
                        pltpu.VMEM((rows, 2 * WIDTH), BF16),
                        pltpu.VMEM((A_GROUPS, CHUNK, CHUNK), BF16),
                        pltpu.VMEM((SUBLANES + rows, B_XBC), F32),
                        pltpu.VMEM((CHUNK, WIDTH), F32),
                        pltpu.VMEM((CHUNK, B_GROUPS * B_STATE), F32),
                        pltpu.VMEM((CHUNK, B_GROUPS * B_STATE), F32),
                        pltpu.VMEM((CHUNK, WIDTH), F32),
                        pltpu.VMEM((B_HEADS, B_STATE, B_HEAD_DIM), F32)],
        compiler_params=_cparams("arbitrary", "arbitrary"),
    )(h0, h0, h0, h0, h0, dt_raw, x0, w_out, row(next_norm_g), *params)


def _cast_kernel(w_ref, o_ref):
    o_ref[...] = w_ref[...].astype(o_ref.dtype)


def _cast_bf16(w, tr=512):
    r, c = w.shape
    return pl.pallas_call(
        _cast_kernel,
        grid=(r // tr,),
        in_specs=[pl.BlockSpec((tr, c), lambda i: (i, 0))],
        out_specs=pl.BlockSpec((tr, c), lambda i: (i, 0)),
        out_shape=jax.ShapeDtypeStruct((r, c), BF16),
        compiler_params=_cparams("arbitrary"),
    )(w)


def _shortconv_kernel(bg_ref, cg_ref, hx_ref, zc_ref, cw_ref, o_ref, xbuf_ref):
    tile = o_ref.shape[0]

    @pl.when(pl.program_id(1) == 0)
    def _():
        xbuf_ref[0:SUBLANES, :] = jnp.zeros((SUBLANES, WIDTH), F32)

    cwid, rows = 256, 128
    for j in range(WIDTH // cwid):
        sl = slice(j * cwid, (j + 1) * cwid)
        for i in range(tile // rows):
            r0 = i * rows
            xbuf_ref[SUBLANES + r0:SUBLANES + r0 + rows, sl] = (
                cg_ref[r0:r0 + rows, sl].astype(F32) * hx_ref[r0:r0 + rows, sl].astype(F32))
        for i in range(tile // rows):
            r0 = i * rows
            acc = None
            for k in range(C_CONV):
                lo = SUBLANES - (C_CONV - 1) + k + r0
                term = xbuf_ref[lo:lo + rows, sl] * cw_ref[k:k + 1, sl]
                acc = term if acc is None else acc + term
            gate = _silu(zc_ref[r0:r0 + rows, sl].astype(F32)) * bg_ref[r0:r0 + rows, sl].astype(F32)
            o_ref[r0:r0 + rows, sl] = (gate * acc).astype(o_ref.dtype)
        xbuf_ref[0:SUBLANES, sl] = xbuf_ref[tile:tile + SUBLANES, sl]


def _shortconv(h1, conv_w, batch, seq, tile=512):
    t = batch * seq
    nt = seq // tile
    tok = lambda cb: (lambda b, i: (b * nt + i, cb))
    return pl.pallas_call(
        _shortconv_kernel,
        grid=(batch, nt),
        in_specs=[pl.BlockSpec((tile, WIDTH), tok(0)), pl.BlockSpec((tile, WIDTH), tok(1)),
                  pl.BlockSpec((tile, WIDTH), tok(2)), pl.BlockSpec((tile, WIDTH), tok(3)),
                  pl.BlockSpec(conv_w.shape, lambda b, i: (0, 0))],
        out_specs=pl.BlockSpec((tile, WIDTH), tok(0)),
        out_shape=jax.ShapeDtypeStruct((t, WIDTH), BF16),
        scratch_shapes=[pltpu.VMEM((SUBLANES + tile, WIDTH), F32)],
        compiler_params=_cparams("arbitrary", "arbitrary"),
    )(h1, h1, h1, h1, conv_w)


R4 = DILATIONS[1]
ROWS4 = SUPER // R4


def _attn4_kernel(q_ref, kc_ref, vc_ref, zd_ref, o_ref,
                  qn_ref, kn_ref, vn_ref, q4_ref, k4_ref, v4_ref, os_ref, ls_ref, ms_ref, fin_ref, bias_ref):
    first_super = pl.program_id(2) == 0
    cur0 = pl.multiple_of((pl.program_id(2) % 2) * SUPER, SUPER)
    prev0 = pl.multiple_of(SUPER - cur0, SUPER)
    nq = N_BACK
    blocks = SUPER // nq
    a = lax.broadcasted_iota(jnp.int32, (nq, 2 * nq), 0)
    j = lax.broadcasted_iota(jnp.int32, (nq, 2 * nq), 1)
    in_band = jnp.logical_and(j >= a, j <= a + nq)
    bias_ref[0] = jnp.where(in_band, 0.0, -jnp.inf)
    bias_ref[1] = jnp.where(jnp.logical_and(in_band, j >= nq), 0.0, -jnp.inf)
    qscale = D_HEAD_DIM ** -0.5 * 1.4426950408889634
    ones_cols = jnp.ones((2 * nq, LANES), BF16)

    @pl.when(first_super)
    def _():
        for ref in (kn_ref, vn_ref, k4_ref, v4_ref):
            ref[pl.ds(prev0, SUPER), :] = jnp.zeros((SUPER, LANES), F32)

    qn_ref[...] = q_ref[...].astype(F32) * qscale
    kn_ref[pl.ds(cur0, SUPER), :] = kc_ref[...].astype(F32)
    vn_ref[pl.ds(cur0, SUPER), :] = vc_ref[...].astype(F32)
    for r in range(R4):
        q4_ref[r * ROWS4:(r + 1) * ROWS4] = qn_ref[pl.ds(r, ROWS4, stride=R4), :]
        dst = pl.ds(pl.multiple_of(cur0 + r * ROWS4, ROWS4), ROWS4)
        k4_ref[dst, :] = kn_ref[pl.ds(cur0 + r, ROWS4, stride=R4), :]
        v4_ref[dst, :] = vn_ref[pl.ds(cur0 + r, ROWS4, stride=R4), :]

    def softmax_block(qf, kf, vf, no_prev):
        bias = bias_ref[0] if no_prev is None else bias_ref[no_prev.astype(jnp.int32)]
        s = lax.dot_general(qf.astype(BF16), kf.astype(BF16), (((1,), (1,)), ((), ())),
                            preferred_element_type=F32) + bias
        m = jnp.max(s, axis=-1, keepdims=True)
        p = jnp.exp2(s - m).astype(BF16)
        o2 = jnp.dot(p, jnp.concatenate([vf.astype(BF16), ones_cols], axis=1), preferred_element_type=F32)
        return o2[:, :LANES], o2[:, LANES:], jnp.broadcast_to(m, (nq, LANES))

    def rows_of(base, off):
        return pl.ds(pl.multiple_of(base + off, nq), nq)

    never = None
    for i in range(blocks):
        q1 = pl.ds(i * nq, nq)
        kc1 = rows_of(cur0, i * nq)
        kp1 = rows_of(cur0, (i - 1) * nq) if i > 0 else rows_of(prev0, SUPER - nq)
        in1 = (qn_ref[q1, :],
               jnp.concatenate([kn_ref[kp1, :], kn_ref[kc1, :]], axis=0),
               jnp.concatenate([vn_ref[kp1, :], vn_ref[kc1, :]], axis=0),
               first_super if i == 0 else never)
        res, sub = divmod(i, ROWS4 // nq)
        q4 = pl.ds(res * ROWS4 + sub * nq, nq)
        kc4 = rows_of(cur0, res * ROWS4 + sub * nq)
        kp4 = rows_of(cur0, res * ROWS4 + (sub - 1) * nq) if sub > 0 else rows_of(prev0, res * ROWS4 + ROWS4 - nq)
        in4 = (q4_ref[q4, :],
               jnp.concatenate([k4_ref[kp4, :], k4_ref[kc4, :]], axis=0),
               jnp.concatenate([v4_ref[kp4, :], v4_ref[kc4, :]], axis=0),
               first_super if sub == 0 else never)
        hi, lo = divmod(i, R4)
        q16 = pl.ds(lo * ROWS4 + hi, nq, stride=R4)
        kp16 = pl.ds(prev0 + lo * ROWS4 + hi, nq, stride=R4)
        kc16 = pl.ds(cur0 + lo * ROWS4 + hi, nq, stride=R4)
        in16 = (q4_ref[q16, :],
                jnp.concatenate([k4_ref[kp16, :], k4_ref[kc16, :]], axis=0),
                jnp.concatenate([v4_ref[kp16, :], v4_ref[kc16, :]], axis=0),
                first_super)
        outs = [softmax_block(*args) for args in (in1, in4, in16)]
        for pi, qsl in enumerate((q1, q4, q16)):
            os_ref[pi, qsl, :] = outs[pi][0]
            ls_ref[pi, qsl, :] = outs[pi][1]
            ms_ref[pi, qsl, :] = outs[pi][2]

    rows = 256

    def combine(c, carry):
        home = pl.ds(pl.multiple_of(c * rows, rows), rows)
        r, part = c // (ROWS4 // rows), c % (ROWS4 // rows)
        tok = pl.ds(r + part * (rows * R4), rows, stride=R4)
        ms = [ms_ref[0, tok, :], ms_ref[1, home, :], ms_ref[2, home, :]]
        os = [os_ref[0, tok, :], os_ref[1, home, :], os_ref[2, home, :]]
        ls = [ls_ref[0, tok, :], ls_ref[1, home, :], ls_ref[2, home, :]]
        mx = jnp.maximum(jnp.maximum(ms[0], ms[1]), ms[2])
        num = jnp.zeros((rows, LANES), F32)
        den = jnp.zeros((rows, LANES), F32)
        for pi in range(len(DILATIONS)):
            e = jnp.exp2(ms[pi] - mx)
            num = num + e * os[pi]
            den = den + e * ls[pi]
        fin_ref[tok, :] = num / den
        return carry

    lax.fori_loop(0, SUPER // rows, combine, 0)

    def gate(c, carry):
        rs = pl.ds(pl.multiple_of(c * rows, rows), rows)
        o_ref[rs, :] = (_silu(zd_ref[rs, :].astype(F32)) * fin_ref[rs, :]).astype(o_ref.dtype)
        return carry

    lax.fori_loop(0, SUPER // rows, gate, 0)


def _attention4(h1, col0, batch, seq):
    t = batch * seq
    ns = seq // SUPER
    hw = D_HEAD_DIM
    cb0 = col0 // hw
    per = WIDTH // hw
    npat = len(DILATIONS)

    def cur(part):
        return lambda b, g, s: (b * ns + s, cb0 + part * per + g)

    blk = (SUPER, hw)
    return pl.pallas_call(
        _attn4_kernel,
        grid=(batch, per, ns),
        in_specs=[pl.BlockSpec(blk, cur(0)), pl.BlockSpec(blk, cur(1)), pl.BlockSpec(blk, cur(2)),
                  pl.BlockSpec(blk, cur(3))],
        out_specs=pl.BlockSpec(blk, lambda b, g, s: (b * ns + s, g)),
        out_shape=jax.ShapeDtypeStruct((t, WIDTH), BF16),
        scratch_shapes=[pltpu.VMEM((SUPER, LANES), F32),
                        pltpu.VMEM((2 * SUPER, LANES), F32),
                        pltpu.VMEM((2 * SUPER, LANES), F32),
                        pltpu.VMEM((SUPER, LANES), F32),
                        pltpu.VMEM((2 * SUPER, LANES), F32),
                        pltpu.VMEM((2 * SUPER, LANES), F32),
                        pltpu.VMEM((npat, SUPER, LANES), F32),
                        pltpu.VMEM((npat, SUPER, LANES), F32),
                        pltpu.VMEM((npat, SUPER, LANES), F32),
                        pltpu.VMEM((SUPER, LANES), F32),
                        pltpu.VMEM((2, N_BACK, 2 * N_BACK), F32)],
        compiler_params=_cparams("arbitrary", "arbitrary", "arbitrary"),
    )(h1, h1, h1, h1)


def kernel(x, even_norm_g, even_w_in, gmlp_ln_g, gmlp_ln_b, gmlp_ws, gmlp_bs, ssd_conv_w, ssd_conv_b,
           ssd_dt_bias, ssd_a_log, ssd_d, ssd_norm_g, even_w_out, odd_norm_g, odd_w_in, sconv_w,
           odd_w_out, final_norm_g):
    batch, seq, d = x.shape
    assert d == D_MODEL and seq % SUPER == 0
    assert even_norm_g.shape[0] == 1 and odd_norm_g.shape[0] == 1
    t = batch * seq
    x0 = x.reshape(t, d)
    w_in0_t = jnp.swapaxes(even_w_in, 1, 2).reshape(-1, d)
    w_out0 = even_w_out.reshape(-1, d)
    w_in1 = odd_w_in.reshape(d, -1)
    w_out1 = odd_w_out.reshape(-1, d)

    n_main = 4 * WIDTH + B_XBC
    w_dt = jnp.pad(w_in0_t[n_main:, :].T, ((0, 0), (0, LANES - B_HEADS)))
    xn0, dt_raw = _rmsnorm_dt(x0, even_norm_g[0], w_dt)
    h0 = _matmul(xn0, w_in0_t, n_main, w_is_nk=True)
    x1, xn1 = _mix0(h0, dt_raw, x0, _cast_bf16(w_out0), odd_norm_g[0], batch, seq,
                    gmlp_ln_g[0], gmlp_ln_b[0], gmlp_ws[0], gmlp_bs[0],
                    ssd_conv_w[0], ssd_conv_b[0], ssd_dt_bias[0], ssd_a_log[0], ssd_d[0], ssd_norm_g[0])

    h1 = _matmul(xn1, w_in1, w_in1.shape[1])
    yc = _shortconv(h1, sconv_w[0], batch, seq)
    yd = _attention4(h1, 4 * WIDTH, batch, seq)
    return _outproj_norm(yc, yd, _cast_bf16(w_out1), x1, final_norm_g).reshape(batch, seq, d)
```

```python
import functools

import jax
import jax.numpy as jnp
from jax import lax
from jax.experimental import pallas as pl
from jax.experimental.pallas import tpu as pltpu

F32 = jnp.float32
BF16 = jnp.bfloat16

EPS = 1e-5
D_MODEL = 2048
WIDTH = 2048
A_GROUPS = 8
CHUNK = 128
MIX0_CHUNKS = 1
B_HEAD_DIM = 64
B_HEADS = WIDTH // B_HEAD_DIM
B_GROUPS = 8
B_STATE = 128
B_CONV = 4
B_XBC = WIDTH + 2 * B_GROUPS * B_STATE
HEADS_PER_GROUP = B_HEADS // B_GROUPS
GROUP_W = WIDTH // B_GROUPS
C_CONV = 3
D_HEAD_DIM = 128
D_HEADS = WIDTH // D_HEAD_DIM
N_BACK = 128
DILATIONS = (1, 4, 16)
SUPER = N_BACK * DILATIONS[-1]
LANES = 128
SUBLANES = 8
VMEM_LIMIT = 56 * 1024 * 1024


def _cparams(*sem):
    return pltpu.CompilerParams(dimension_semantics=sem, vmem_limit_bytes=VMEM_LIMIT)


def _silu(z):
    hz = 0.5 * z
    return hz * jnp.tanh(hz) + hz


def _rmsnorm_dt_kernel(x_ref, g_ref, w_ref, o_ref, dt_ref):
    x = x_ref[...]
    ms = jnp.mean(x * x, axis=-1, keepdims=True)
    xn = (x * lax.rsqrt(ms + EPS) * g_ref[...]).astype(o_ref.dtype)
    o_ref[...] = xn
    dt_ref[...] = jnp.dot(xn, w_ref[...].astype(BF16), preferred_element_type=F32)


def _rmsnorm_dt(x2d, g, w_dt, tm=512):
    t, d = x2d.shape
    n = w_dt.shape[1]
    return pl.pallas_call(
        _rmsnorm_dt_kernel,
        grid=(t // tm,),
        in_specs=[pl.BlockSpec((tm, d), lambda i: (i, 0)),
                  pl.BlockSpec((1, d), lambda i: (0, 0)),
                  pl.BlockSpec((d, n), lambda i: (0, 0))],
        out_specs=[pl.BlockSpec((tm, d), lambda i: (i, 0)), pl.BlockSpec((tm, n), lambda i: (i, 0))],
        out_shape=[jax.ShapeDtypeStruct((t, d), BF16), jax.ShapeDtypeStruct((t, n), F32)],
        compiler_params=_cparams("arbitrary"),
    )(x2d, g.reshape(1, d), w_dt)


def _outproj_norm_kernel(a1_ref, a2_ref, w_ref, r_ref, g_ref, o_ref):
    k1 = a1_ref.shape[1]
    x = (jnp.dot(a1_ref[...], w_ref[0:k1, :], preferred_element_type=F32)
         + jnp.dot(a2_ref[...], w_ref[k1:, :], preferred_element_type=F32) + r_ref[...])
    ms = jnp.mean(x * x, axis=-1, keepdims=True)
    o_ref[...] = x * lax.rsqrt(ms + EPS) * g_ref[...]


def _outproj_norm(a1, a2, w_bf16, res, g, tm=512):
    m, k1 = a1.shape
    k2 = a2.shape[1]
    n = w_bf16.shape[1]
    return pl.pallas_call(
        _outproj_norm_kernel,
        grid=(m // tm,),
        in_specs=[pl.BlockSpec((tm, k1), lambda i: (i, 0)),
                  pl.BlockSpec((tm, k2), lambda i: (i, 0)),
                  pl.BlockSpec(w_bf16.shape, lambda i: (0, 0), pipeline_mode=pl.Buffered(1)),
                  pl.BlockSpec((tm, n), lambda i: (i, 0)),
                  pl.BlockSpec((1, n), lambda i: (0, 0))],
        out_specs=pl.BlockSpec((tm, n), lambda i: (i, 0)),
        out_shape=jax.ShapeDtypeStruct((m, n), F32),
        compiler_params=_cparams("arbitrary"),
    )(a1, a2, w_bf16, res, g.reshape(1, n))


def _matmul_kernel(a_ref, w_ref, o_ref, wb_ref, *, cast_rows, w_is_nk):
    @pl.when(pl.program_id(1) == 0)
    def _():
        def body(k, carry):
            r = pl.multiple_of(k * cast_rows, cast_rows)
            wb_ref[pl.ds(r, cast_rows), :] = w_ref[pl.ds(r, cast_rows), :].astype(BF16)
            return carry
        lax.fori_loop(0, w_ref.shape[0] // cast_rows, body, 0)

    dims = (((1,), (1,)), ((), ())) if w_is_nk else (((1,), (0,)), ((), ()))
    o_ref[...] = lax.dot_general(a_ref[...], wb_ref[...], dims, preferred_element_type=F32).astype(o_ref.dtype)


def _matmul(a, w, n_cols, w_is_nk=False, tm=2048, tn=1024):
    m, k = a.shape
    if w_is_nk:
        w_block = pl.BlockSpec((tn, k), lambda j, i: (j, 0))
    else:
        w_block = pl.BlockSpec((k, tn), lambda j, i: (0, j))
    return pl.pallas_call(
        functools.partial(_matmul_kernel, cast_rows=256, w_is_nk=w_is_nk),
        grid=(n_cols // tn, m // tm),
        in_specs=[pl.BlockSpec((tm, k), lambda j, i: (i, 0)), w_block],
        out_specs=pl.BlockSpec((tm, tn), lambda j, i: (i, j)),
        out_shape=jax.ShapeDtypeStruct((m, n_cols), BF16),
        scratch_shapes=[pltpu.VMEM(w_block.block_shape, BF16)],
        compiler_params=_cparams("arbitrary", "arbitrary"),
    )(a, w)


def _softplus(x):
    return jnp.maximum(x, 0.0) + jnp.log1p(jnp.exp(-jnp.abs(x)))


def _split3(x):
    hi = x.astype(BF16)
    r1 = x - hi.astype(F32)
    mid = r1.astype(BF16)
    lo = (r1 - mid.astype(F32)).astype(BF16)
    return hi, mid, lo


def _mix0_kernel(u_ref, v_ref, za_ref, zb_ref, xbc_ref, dt_ref, xres_ref, wout_ref, g1_ref,
                 lng_ref, lnb_ref, ws_ref, bst_ref, cw_ref, cbias_ref, dtb_ref, alog_ref, dexp_ref, ng_ref,
                 x1_ref, xn1_ref,
                 o_ref, yprev_ref, wsb_ref, xbuf_ref, xs_ref, bm_ref, cm_ref, y_ref, st_ref):
    q = CHUNK
    row = lax.broadcasted_iota(jnp.int32, (q, q), 0)
    col = lax.broadcasted_iota(jnp.int32, (q, q), 1)
    causal = col <= row

    @pl.when(pl.program_id(1) == 0)
    def _():
        o_ref[...] = jnp.zeros_like(o_ref)
        xbuf_ref[0:SUBLANES, :] = jnp.zeros((SUBLANES, B_XBC), F32)
        st_ref[...] = jnp.zeros_like(st_ref)
        for g in range(A_GROUPS):
            wsb_ref[g] = jnp.where(causal, ws_ref[g], 0.0).astype(BF16)

    yprev_ref[...] = o_ref[...]

    tril = jnp.where(causal, 1.0, 0.0).astype(BF16)
    chunk_args = (causal, tril, u_ref, v_ref, za_ref, zb_ref, xbc_ref, dt_ref,
                  lng_ref, lnb_ref, bst_ref, cw_ref, cbias_ref, dtb_ref, alog_ref, dexp_ref, ng_ref,
                  o_ref, wsb_ref, xbuf_ref, xs_ref, bm_ref, cm_ref, y_ref, st_ref)
    chunks = [slice(ci * q, (ci + 1) * q) for ci in range(o_ref.shape[0] // q)]
    for rs in chunks:
        _mix0_chunk(rs, "conv", *chunk_args)
        _mix0_chunk(rs, "scan", *chunk_args)
    rows = o_ref.shape[0]
    xbuf_ref[0:SUBLANES, :] = xbuf_ref[rows:rows + SUBLANES, :]

    ncol = 512
    ss = jnp.zeros((o_ref.shape[0], 1), F32)
    for nb in range(D_MODEL // ncol):
        cs = slice(nb * ncol, (nb + 1) * ncol)
        xc = jnp.dot(yprev_ref[...], wout_ref[:, cs], preferred_element_type=F32) + xres_ref[:, cs]
        x1_ref[:, cs] = xc
        ss = ss + jnp.sum(xc * xc, axis=-1, keepdims=True)
    rstd1 = lax.rsqrt(ss * (1.0 / D_MODEL) + EPS)
    for nb in range(D_MODEL // ncol):
        cs = slice(nb * ncol, (nb + 1) * ncol)
        xn1_ref[:, cs] = (x1_ref[:, cs] * rstd1 * g1_ref[:, cs]).astype(xn1_ref.dtype)

    for rs in chunks:
        _mix0_chunk(rs, "gmlp", *chunk_args)


def _mix0_chunk(rs, part, causal, tril, u_ref, v_ref, za_ref, zb_ref, xbc_ref, dt_ref,
                lng_ref, lnb_ref, bst_ref, cw_ref, cbias_ref, dtb_ref, alog_ref, dexp_ref, ng_ref,
                o_ref, wsb_ref, xbuf_ref, xs_ref, bm_ref, cm_ref, y_ref, st_ref):
    q = CHUNK
    if part == "gmlp":
        _gmlp_chunk(rs, u_ref, v_ref, za_ref, lng_ref, lnb_ref, bst_ref, o_ref, wsb_ref)
        return
    if part == "scan":
        _scan_chunk(rs, causal, tril, zb_ref, dt_ref, dtb_ref, alog_ref, dexp_ref, ng_ref,
                    o_ref, xs_ref, bm_ref, cm_ref, y_ref, st_ref)
        return
    cwid = 256
    base = SUBLANES + rs.start
    for j in range(B_XBC // cwid):
        sl = slice(j * cwid, (j + 1) * cwid)
        xbuf_ref[base:base + q, sl] = xbc_ref[rs, sl].astype(F32)
        acc = cbias_ref[:, sl]
        for k in range(B_CONV):
            lo = base - (B_CONV - 1) + k
            acc = acc + xbuf_ref[lo:lo + q, sl] * cw_ref[k:k + 1, sl]
        act = _silu(acc)
        lo = j * cwid
        if lo < WIDTH:
            xs_ref[:, lo:lo + cwid] = act
        elif lo < WIDTH + B_GROUPS * B_STATE:
            bm_ref[:, lo - WIDTH:lo - WIDTH + cwid] = act
        else:
            off = lo - WIDTH - B_GROUPS * B_STATE
            cm_ref[:, off:off + cwid] = act


def _scan_chunk(rs, causal, tril, zb_ref, dt_ref, dtb_ref, alog_ref, dexp_ref, ng_ref,
                o_ref, xs_ref, bm_ref, cm_ref, y_ref, st_ref):
    q = CHUNK
    dt = _softplus(dt_ref[rs, :] + dtb_ref[...])
    adt = dt * (-jnp.exp(alog_ref[...]))
    a_cs = sum(jnp.dot(tril, part, preferred_element_type=F32) for part in _split3(adt))
    a_cs_t = a_cs.T
    dt_t = dt.T

    for g in range(B_GROUPS):
        gs = slice(g * B_STATE, (g + 1) * B_STATE)
        bg = bm_ref[:, gs]
        cgb = cm_ref[:, gs].astype(BF16)
        bg_t = bg.T
        cb = lax.dot_general(cgb, bg.astype(BF16), (((1,), (1,)), ((), ())),
                             preferred_element_type=F32)
        for r in range(HEADS_PER_GROUP):
            h = g * HEADS_PER_GROUP + r
            hs = slice(h * B_HEAD_DIM, (h + 1) * B_HEAD_DIM)
            colv = a_cs[:, h:h + 1]
            rowv = a_cs_t[h:h + 1, :]
            dtr = dt_t[h:h + 1, :]
            decay = jnp.exp(jnp.where(causal, colv - rowv, -jnp.inf))
            mh = (cb * decay * dtr).astype(BF16)
            xh = xs_ref[:, hs]
            xhb = xh.astype(BF16)
            state = st_ref[h]
            y = jnp.dot(mh, xhb, preferred_element_type=F32)
            y = y + jnp.exp(colv) * jnp.dot(cgb, state.astype(BF16), preferred_element_type=F32)
            y_ref[:, hs] = y + dexp_ref[:, hs] * xh
            last = a_cs_t[h:h + 1, q - 1:q]
            wrow = jnp.exp(last - rowv) * dtr
            new = jnp.dot((bg_t * wrow).astype(BF16), xhb, preferred_element_type=F32)
            st_ref[h] = state * jnp.exp(last) + new

        sl = slice(g * GROUP_W, (g + 1) * GROUP_W)
        y = y_ref[:, sl] * _silu(zb_ref[rs, sl].astype(F32))
        ms = jnp.mean(y * y, axis=-1, keepdims=True)
        o_ref[rs, WIDTH + g * GROUP_W:WIDTH + (g + 1) * GROUP_W] = (
            y * lax.rsqrt(ms + EPS) * ng_ref[:, sl]).astype(o_ref.dtype)


def _gmlp_chunk(rs, u_ref, v_ref, za_ref, lng_ref, lnb_ref, bst_ref, o_ref, wsb_ref):
    v = v_ref[rs, :].astype(F32)
    mu = jnp.mean(v, axis=-1, keepdims=True)
    xc = v - mu
    var = jnp.mean(xc * xc, axis=-1, keepdims=True)
    rstd = lax.rsqrt(var + EPS)
    for g in range(A_GROUPS):
        sl = slice(g * GROUP_W, (g + 1) * GROUP_W)
        vg = v_ref[rs, sl].astype(F32)
        vn = ((vg - mu) * rstd * lng_ref[:, sl] + lnb_ref[:, sl]).astype(BF16)
        mixed = jnp.dot(wsb_ref[g], vn, preferred_element_type=F32) + bst_ref[:, g:g + 1]
        z = za_ref[rs, sl].astype(F32)
        u = u_ref[rs, sl].astype(F32)
        o_ref[rs, sl] = (_silu(z) * (u * mixed)).astype(o_ref.dtype)


def _mix0(h0, dt_raw, x0, w_out, next_norm_g, batch, seq,
          lng, lnb, ws, bs, conv_w, conv_b, dt_bias, a_log, d_skip, norm_g):
    t = batch * seq
    rows = MIX0_CHUNKS * CHUNK
    nc = seq // rows
    pad = LANES - B_HEADS
    row = lambda a: a.reshape(1, -1)
    params = [row(lng), row(lnb), ws, bs.T, conv_w, row(conv_b),
              row(jnp.pad(dt_bias, (0, pad))), row(jnp.pad(a_log, (0, pad))),
              row(jnp.repeat(d_skip, B_HEAD_DIM)), row(norm_g)]
    tok = lambda cb: (lambda b, c: (b * nc + jnp.minimum(c, nc - 1), cb))
    lag = lambda b, c: (b * nc + jnp.maximum(c - 1, 0), 0)
    full = lambda a: pl.BlockSpec(a.shape, lambda b, c: (0,) * a.ndim)
    in_specs = [pl.BlockSpec((rows, WIDTH), tok(0)),
                pl.BlockSpec((rows, WIDTH), tok(1)),
                pl.BlockSpec((rows, WIDTH), tok(2)),
                pl.BlockSpec((rows, WIDTH), tok(3)),
                pl.BlockSpec((rows, B_XBC), tok(2)),
                pl.BlockSpec((rows, LANES), tok(0)),
                pl.BlockSpec((rows, D_MODEL), lag),
                pl.BlockSpec(w_out.shape, lambda b, c: (0, 0), pipeline_mode=pl.Buffered(1)),
                full(row(next_norm_g))]
    in_specs += [full(p) for p in params]
    return pl.pallas_call(
        _mix0_kernel,
        grid=(batch, nc + 1),
        in_specs=in_specs,
        out_specs=[pl.BlockSpec((rows, D_MODEL), lag), pl.BlockSpec((rows, D_MODEL), lag)],
        out_shape=[jax.ShapeDtypeStruct((t, D_MODEL), F32), jax.ShapeDtypeStruct((t, D_MODEL), BF16)],
        scratch_shapes=[pltpu.VMEM((rows, 2 * WIDTH), BF16),
                        pltpu.VMEM((rows, 2 * WIDTH), BF16),
                        pltpu.VMEM((A_GROUPS, CHUNK, CHUNK), BF16),
                        pltpu.VMEM((SUBLANES + rows, B_XBC), F32),
                        pltpu.VMEM((CHUNK, WIDTH), F32),
                        pltpu.VMEM((CHUNK, B_GROUPS * B_STATE), F32),
                        pltpu.VMEM((CHUNK, B_GROUPS * B_STATE), F32),
                        pltpu.VMEM((CHUNK, WIDTH), F32),
                        pltpu.VMEM((B_HEADS, B_STATE, B_HEAD_DIM), F32)],
        compiler_params=_cparams("arbitrary", "arbitrary"),
    )(h0, h0, h0, h0, h0, dt_raw, x0, w_out, row(next_norm_g), *params)


def _cast_kernel(w_ref, o_ref):
    o_ref[...] = w_ref[...].astype(o_ref.dtype)


def _cast_bf16(w, tr=512):
    r, c = w.shape
    return pl.pallas_call(
        _cast_kernel,
        grid=(r // tr,),
        in_specs=[pl.BlockSpec((tr, c), lambda i: (i, 0))],
        out_specs=pl.BlockSpec((tr, c), lambda i: (i, 0)),
        out_shape=jax.ShapeDtypeStruct((r, c), BF16),
        compiler_params=_cparams("arbitrary"),
    )(w)


def _shortconv_kernel(bg_ref, cg_ref, hx_ref, zc_ref, cw_ref, o_ref, xbuf_ref):
    tile = o_ref.shape[0]

    @pl.when(pl.program_id(1) == 0)
    def _():
        xbuf_ref[0:SUBLANES, :] = jnp.zeros((SUBLANES, WIDTH), F32)

    cwid, rows = 256, 128
    for j in range(WIDTH // cwid):
        sl = slice(j * cwid, (j + 1) * cwid)
        for i in range(tile // rows):
            r0 = i * rows
            xbuf_ref[SUBLANES + r0:SUBLANES + r0 + rows, sl] = (
                cg_ref[r0:r0 + rows, sl].astype(F32) * hx_ref[r0:r0 + rows, sl].astype(F32))
        for i in range(tile // rows):
            r0 = i * rows
            acc = None
            for k in range(C_CONV):
                lo = SUBLANES - (C_CONV - 1) + k + r0
                term = xbuf_ref[lo:lo + rows, sl] * cw_ref[k:k + 1, sl]
                acc = term if acc is None else acc + term
            gate = _silu(zc_ref[r0:r0 + rows, sl].astype(F32)) * bg_ref[r0:r0 + rows, sl].astype(F32)
            o_ref[r0:r0 + rows, sl] = (gate * acc).astype(o_ref.dtype)
        xbuf_ref[0:SUBLANES, sl] = xbuf_ref[tile:tile + SUBLANES, sl]


def _shortconv(h1, conv_w, batch, seq, tile=512):
    t = batch * seq
    nt = seq // tile
    tok = lambda cb: (lambda b, i: (b * nt + i, cb))
    return pl.pallas_call(
        _shortconv_kernel,
        grid=(batch, nt),
        in_specs=[pl.BlockSpec((tile, WIDTH), tok(0)), pl.BlockSpec((tile, WIDTH), tok(1)),
                  pl.BlockSpec((tile, WIDTH), tok(2)), pl.BlockSpec((tile, WIDTH), tok(3)),
                  pl.BlockSpec(conv_w.shape, lambda b, i: (0, 0))],
        out_specs=pl.BlockSpec((tile, WIDTH), tok(0)),
        out_shape=jax.ShapeDtypeStruct((t, WIDTH), BF16),
        scratch_shapes=[pltpu.VMEM((SUBLANES + tile, WIDTH), F32)],
        compiler_params=_cparams("arbitrary", "arbitrary"),
    )(h1, h1, h1, h1, conv_w)


R4 = DILATIONS[1]
ROWS4 = SUPER // R4


def _attn4_kernel(q_ref, kc_ref, vc_ref, zd_ref, o_ref,
                  qn_ref, kn_ref, vn_ref, q4_ref, k4_ref, v4_ref, os_ref, ls_ref, ms_ref, fin_ref, bias_ref):
    first_super = pl.program_id(2) == 0
    cur0 = pl.multiple_of((pl.program_id(2) % 2) * SUPER, SUPER)
    prev0 = pl.multiple_of(SUPER - cur0, SUPER)
    nq = N_BACK
    blocks = SUPER // nq
    a = lax.broadcasted_iota(jnp.int32, (nq, 2 * nq), 0)
    j = lax.broadcasted_iota(jnp.int32, (nq, 2 * nq), 1)
    in_band = jnp.logical_and(j >= a, j <= a + nq)
    bias_ref[0] = jnp.where(in_band, 0.0, -jnp.inf)
    bias_ref[1] = jnp.where(jnp.logical_and(in_band, j >= nq), 0.0, -jnp.inf)
    qscale = D_HEAD_DIM ** -0.5 * 1.4426950408889634
    ones_cols = jnp.ones((2 * nq, LANES), BF16)

    @pl.when(first_super)
    def _():
        for ref in (kn_ref, vn_ref, k4_ref, v4_ref):
            ref[pl.ds(prev0, SUPER), :] = jnp.zeros((SUPER, LANES), F32)

    qn_ref[...] = q_ref[...].astype(F32) * qscale
    kn_ref[pl.ds(cur0, SUPER), :] = kc_ref[...].astype(F32)
    vn_ref[pl.ds(cur0, SUPER), :] = vc_ref[...].astype(F32)
    for r in range(R4):
        q4_ref[r * ROWS4:(r + 1) * ROWS4] = qn_ref[pl.ds(r, ROWS4, stride=R4), :]
        dst = pl.ds(pl.multiple_of(cur0 + r * ROWS4, ROWS4), ROWS4)
        k4_ref[dst, :] = kn_ref[pl.ds(cur0 + r, ROWS4, stride=R4), :]
        v4_ref[dst, :] = vn_ref[pl.ds(cur0 + r, ROWS4, stride=R4), :]

    def softmax_block(qf, kf, vf, no_prev):
        bias = bias_ref[0] if no_prev is None else bias_ref[no_prev.astype(jnp.int32)]
        s = lax.dot_general(qf.astype(BF16), kf.astype(BF16), (((1,), (1,)), ((), ())),
                            preferred_element_type=F32) + bias
        m = jnp.max(s, axis=-1, keepdims=True)
        p = jnp.exp2(s - m).astype(BF16)
        o2 = jnp.dot(p, jnp.concatenate([vf.astype(BF16), ones_cols], axis=1), preferred_element_type=F32)
        return o2[:, :LANES], o2[:, LANES:], jnp.broadcast_to(m, (nq, LANES))

    def rows_of(base, off):
        return pl.ds(pl.multiple_of(base + off, nq), nq)

    never = None
    for i in range(blocks):
        q1 = pl.ds(i * nq, nq)
        kc1 = rows_of(cur0, i * nq)
        kp1 = rows_of(cur0, (i - 1) * nq) if i > 0 else rows_of(prev0, SUPER - nq)
        in1 = (qn_ref[q1, :],
               jnp.concatenate([kn_ref[kp1, :], kn_ref[kc1, :]], axis=0),
               jnp.concatenate([vn_ref[kp1, :], vn_ref[kc1, :]], axis=0),
               first_super if i == 0 else never)
        res, sub = divmod(i, ROWS4 // nq)
        q4 = pl.ds(res * ROWS4 + sub * nq, nq)
        kc4 = rows_of(cur0, res * ROWS4 + sub * nq)
        kp4 = rows_of(cur0, res * ROWS4 + (sub - 1) * nq) if sub > 0 else rows_of(prev0, res * ROWS4 + ROWS4 - nq)
        in4 = (q4_ref[q4, :],
               jnp.concatenate([k4_ref[kp4, :], k4_ref[kc4, :]], axis=0),
               jnp.concatenate([v4_ref[kp4, :], v4_ref[kc4, :]], axis=0),
               first_super if sub == 0 else never)
        hi, lo = divmod(i, R4)
        q16 = pl.ds(lo * ROWS4 + hi, nq, stride=R4)
        kp16 = pl.ds(prev0 + lo * ROWS4 + hi, nq, stride=R4)
        kc16 = pl.ds(cur0 + lo * ROWS4 + hi, nq, stride=R4)
        in16 = (q4_ref[q16, :],
                jnp.concatenate([k4_ref[kp16, :], k4_ref[kc16, :]], axis=0),
                jnp.concatenate([v4_ref[kp16, :], v4_ref[kc16, :]], axis=0),
                first_super)
        outs = [softmax_block(*args) for args in (in1, in4, in16)]
        for pi, qsl in enumerate((q1, q4, q16)):
            os_ref[pi, qsl, :] = outs[pi][0]
            ls_ref[pi, qsl, :] = outs[pi][1]
            ms_ref[pi, qsl, :] = outs[pi][2]

    rows = 256

    def combine(c, carry):
        home = pl.ds(pl.multiple_of(c * rows, rows), rows)
        r, part = c // (ROWS4 // rows), c % (ROWS4 // rows)
        tok = pl.ds(r + part * (rows * R4), rows, stride=R4)
        ms = [ms_ref[0, tok, :], ms_ref[1, home, :], ms_ref[2, home, :]]
        os = [os_ref[0, tok, :], os_ref[1, home, :], os_ref[2, home, :]]
        ls = [ls_ref[0, tok, :], ls_ref[1, home, :], ls_ref[2, home, :]]
        mx = jnp.maximum(jnp.maximum(ms[0], ms[1]), ms[2])
        num = jnp.zeros((rows, LANES), F32)
        den = jnp.zeros((rows, LANES), F32)
        for pi in range(len(DILATIONS)):
            e = jnp.exp2(ms[pi] - mx)
            num = num + e * os[pi]
            den = den + e * ls[pi]
        fin_ref[tok, :] = num / den
        return carry

    lax.fori_loop(0, SUPER // rows, combine, 0)

    def gate(c, carry):
        rs = pl.ds(pl.multiple_of(c * rows, rows), rows)
        o_ref[rs, :] = (_silu(zd_ref[rs, :].astype(F32)) * fin_ref[rs, :]).astype(o_ref.dtype)
        return carry

    lax.fori_loop(0, SUPER // rows, gate, 0)


def _attention4(h1, col0, batch, seq):
    t = batch * seq
    ns = seq // SUPER
    hw = D_HEAD_DIM
    cb0 = col0 // hw
    per = WIDTH // hw
    npat = len(DILATIONS)

    def cur(part):
        return lambda b, g, s: (b * ns + s, cb0 + part * per + g)

    blk = (SUPER, hw)
    return pl.pallas_call(
        _attn4_kernel,
        grid=(batch, per, ns),
        in_specs=[pl.BlockSpec(blk, cur(0)), pl.BlockSpec(blk, cur(1)), pl.BlockSpec(blk, cur(2)),
                  pl.BlockSpec(blk, cur(3))],
        out_specs=pl.BlockSpec(blk, lambda b, g, s: (b * ns + s, g)),
        out_shape=jax.ShapeDtypeStruct((t, WIDTH), BF16),
        scratch_shapes=[pltpu.VMEM((SUPER, LANES), F32),
                        pltpu.VMEM((2 * SUPER, LANES), F32),
                        pltpu.VMEM((2 * SUPER, LANES), F32),
                        pltpu.VMEM((SUPER, LANES), F32),
                        pltpu.VMEM((2 * SUPER, LANES), F32),
                        pltpu.VMEM((2 * SUPER, LANES), F32),
                        pltpu.VMEM((npat, SUPER, LANES), F32),
                        pltpu.VMEM((npat, SUPER, LANES), F32),
                        pltpu.VMEM((npat, SUPER, LANES), F32),
                        pltpu.VMEM((SUPER, LANES), F32),
                        pltpu.VMEM((2, N_BACK, 2 * N_BACK), F32)],
        compiler_params=_cparams("arbitrary", "arbitrary", "arbitrary"),
    )(h1, h1, h1, h1)


def kernel(x, even_norm_g, even_w_in, gmlp_ln_g, gmlp_ln_b, gmlp_ws, gmlp_bs, ssd_conv_w, ssd_conv_b,
           ssd_dt_bias, ssd_a_log, ssd_d, ssd_norm_g, even_w_out, odd_norm_g, odd_w_in, sconv_w,
           odd_w_out, final_norm_g):
    batch, seq, d = x.shape
    assert d == D_MODEL and seq % SUPER == 0
    assert even_norm_g.shape[0] == 1 and odd_norm_g.shape[0] == 1
    t = batch * seq
    x0 = x.reshape(t, d)
    w_in0_t = jnp.swapaxes(even_w_in, 1, 2).reshape(-1, d)
    w_out0 = even_w_out.reshape(-1, d)
    w_in1 = odd_w_in.reshape(d, -1)
    w_out1 = odd_w_out.reshape(-1, d)

    n_main = 4 * WIDTH + B_XBC
    w_dt = jnp.pad(w_in0_t[n_main:, :].T, ((0, 0), (0, LANES - B_HEADS)))
    xn0, dt_raw = _rmsnorm_dt(x0, even_norm_g[0], w_dt)
    h0 = _matmul(xn0, w_in0_t, n_main, w_is_nk=True)
    x1, xn1 = _mix0(h0, dt_raw, x0, _cast_bf16(w_out0), odd_norm_g[0], batch, seq,
                    gmlp_ln_g[0], gmlp_ln_b[0], gmlp_ws[0], gmlp_bs[0],
                    ssd_conv_w[0], ssd_conv_b[0], ssd_dt_bias[0], ssd_a_log[0], ssd_d[0], ssd_norm_g[0])

    h1 = _matmul(xn1, w_in1, w_in1.shape[1])
    yc = _shortconv(h1, sconv_w[0], batch, seq)
    yd = _attention4(h1, 4 * WIDTH, batch, seq)
    return _outproj_norm(yc, yd, _cast_bf16(w_out1), x1, final_norm_g).reshape(batch, seq, d)
```
